```python
import jax, jax.numpy as jnp
from jax import lax
import numpy as np

D_MODEL = 2048
BATCH = 1
SEQ = 16384
DEPTH = 1
DEC_BATCH = 2
DEC_SEQ = 16384
PAST_LEN = 128

HEAD_DIM = 128
N_HEADS = 8
N_KV_HEADS = 2
ATTN_DIM = N_HEADS * HEAD_DIM
KV_DIM = N_KV_HEADS * HEAD_DIM
CONV_DIM = D_MODEL - ATTN_DIM
CONV_WIDTH = 31
CONV_PAD = CONV_WIDTH // 2
IN_DIM = ATTN_DIM + 2 * KV_DIM + 2 * CONV_DIM
D_FF = ((8 * D_MODEL + 3 * 256 - 1) // (3 * 256)) * 256
GRID_W = 64
ROPE_THETA = 10000.0
ROPE_AXIS_DIM = HEAD_DIM // 2
Q_BLOCK = 128
EPS = 1e-6

kernel_name = "hybrid_gqa_conformer_conv_encoder"


def rms_norm(x, g):
    xf = x.astype(jnp.float32)
    y = xf * lax.rsqrt(jnp.mean(xf * xf, axis=-1, keepdims=True) + EPS)
    return (y * g.astype(jnp.float32)).astype(x.dtype)


def axial_rope_tables(S):
    rows = S // GRID_W
    row = jnp.repeat(jnp.arange(rows, dtype=jnp.float32), GRID_W)
    col = jnp.tile(jnp.arange(GRID_W, dtype=jnp.float32), rows)
    n_freq = ROPE_AXIS_DIM // 2
    inv = ROPE_THETA ** (-(jnp.arange(n_freq, dtype=jnp.float32) * 2.0 / ROPE_AXIS_DIM))
    ang_r = row[:, None] * inv[None, :]
    ang_c = col[:, None] * inv[None, :]
    return jnp.cos(ang_r), jnp.sin(ang_r), jnp.cos(ang_c), jnp.sin(ang_c)


def _rot(v, cos, sin):
    half = v.shape[-1] // 2
    v1, v2 = v[..., :half], v[..., half:]
    c = cos[None, :, None, :]
    s = sin[None, :, None, :]
    return jnp.concatenate([v1 * c - v2 * s, v2 * c + v1 * s], axis=-1)


def apply_axial_rope(x, tabs):
    cos_r, sin_r, cos_c, sin_c = tabs
    xf = x.astype(jnp.float32)
    out = jnp.concatenate([_rot(xf[..., :ROPE_AXIS_DIM], cos_r, sin_r),
                           _rot(xf[..., ROPE_AXIS_DIM:], cos_c, sin_c)], axis=-1)
    return out.astype(x.dtype)


def blocked_gqa_attention(q, k, v):
    B, S, H, D = q.shape
    G = H // N_KV_HEADS
    nb = S // Q_BLOCK
    scale = 1.0 / np.sqrt(D)
    qb = q.reshape(B, nb, Q_BLOCK, N_KV_HEADS, G, D).transpose(1, 0, 2, 3, 4, 5)

    def one_block(qblk):
        s = jnp.einsum('bqkgd,bskd->bkgqs', qblk, k).astype(jnp.float32) * scale
        p = jax.nn.softmax(s, axis=-1).astype(v.dtype)
        return jnp.einsum('bkgqs,bskd->bqkgd', p, v)

    o = lax.map(one_block, qb)
    return o.transpose(1, 0, 2, 3, 4, 5).reshape(B, S, H * D)


def conformer_conv(u, w_dw, b_dw, g_cn, b_cn):
    a, gt = jnp.split(u, 2, axis=-1)
    h = a * jax.nn.sigmoid(gt)
    h = lax.conv_general_dilated(h, w_dw[:, None, :].astype(h.dtype), window_strides=(1,),
                                 padding=[(CONV_PAD, CONV_PAD)],
                                 dimension_numbers=('NWC', 'WIO', 'NWC'),
                                 feature_group_count=CONV_DIM) + b_dw
    hf = h.astype(jnp.float32)
    mu = jnp.mean(hf, axis=-1, keepdims=True)
    var = jnp.mean(jnp.square(hf - mu), axis=-1, keepdims=True)
    y = (hf - mu) * lax.rsqrt(var + EPS) * g_cn.astype(jnp.float32) + b_cn.astype(jnp.float32)
    return (y * jax.nn.sigmoid(y)).astype(u.dtype)


def encoder_layer(x, c, tabs, w_ada, b_ada, g_norm1, w_in, b_in, g_q, g_k, w_dw, b_dw,
                  g_cn, b_cn, w_out, g_norm2, w_gate, w_up, w_down):
    B, S, _ = x.shape
    mod = jax.nn.silu(c) @ w_ada + b_ada
    sh1, sc1, gt1, sh2, sc2, gt2 = [m[:, None, :] for m in jnp.split(mod, 6, axis=-1)]

    h = rms_norm(x, g_norm1) * (1.0 + sc1) + sh1
    z = h @ w_in + b_in
    q = z[..., :ATTN_DIM].reshape(B, S, N_HEADS, HEAD_DIM)
    k = z[..., ATTN_DIM:ATTN_DIM + KV_DIM].reshape(B, S, N_KV_HEADS, HEAD_DIM)
    v = z[..., ATTN_DIM + KV_DIM:ATTN_DIM + 2 * KV_DIM].reshape(B, S, N_KV_HEADS, HEAD_DIM)
    u = z[..., ATTN_DIM + 2 * KV_DIM:]
    q = apply_axial_rope(rms_norm(q, g_q), tabs)
    k = apply_axial_rope(rms_norm(k, g_k), tabs)
    attn = blocked_gqa_attention(q, k, v)
    conv = conformer_conv(u, w_dw, b_dw, g_cn, b_cn)
    mix = jnp.concatenate([attn, conv], axis=-1) @ w_out
    x = x + gt1 * mix

    h = rms_norm(x, g_norm2) * (1.0 + sc2) + sh2
    f = (jax.nn.silu(h @ w_gate) * (h @ w_up)) @ w_down
    return x + gt2 * f


def setup_inputs(seed: int = 0) -> dict:
    key = jax.random.key(seed)
    ks = jax.random.split(key, 24)
    f32 = jnp.float32
    nrm = lambda k, shp, s: jax.random.normal(k, shp, f32) * s
    return {
        "x_prompt": nrm(ks[0], (BATCH, SEQ, D_MODEL), 1.0),
        "x_sample": nrm(ks[1], (DEC_BATCH, DEC_SEQ, D_MODEL), 1.0),
        "c_prompt": nrm(ks[2], (BATCH, D_MODEL), 1.0),
        "c_sample": nrm(ks[3], (DEC_BATCH, D_MODEL), 1.0),
        "w_ada": nrm(ks[4], (DEPTH, D_MODEL, 6 * D_MODEL), 0.2 * D_MODEL ** -0.5),
        "b_ada": nrm(ks[5], (DEPTH, 6 * D_MODEL), 0.02),
        "g_norm1": 1.0 + nrm(ks[6], (DEPTH, D_MODEL), 0.02),
        "w_in": nrm(ks[7], (DEPTH, D_MODEL, IN_DIM), D_MODEL ** -0.5),
        "b_in": nrm(ks[8], (DEPTH, IN_DIM), 0.02),
        "g_q": 1.0 + nrm(ks[9], (DEPTH, HEAD_DIM), 0.02),
        "g_k": 1.0 + nrm(ks[10], (DEPTH, HEAD_DIM), 0.02),
        "w_dw": nrm(ks[11], (DEPTH, CONV_WIDTH, CONV_DIM), CONV_WIDTH ** -0.5),
        "b_dw": nrm(ks[12], (DEPTH, CONV_DIM), 0.02),
        "g_cn": 1.0 + nrm(ks[13], (DEPTH, CONV_DIM), 0.02),
        "b_cn": nrm(ks[14], (DEPTH, CONV_DIM), 0.02),
        "w_out": nrm(ks[15], (DEPTH, D_MODEL, D_MODEL), D_MODEL ** -0.5),
        "g_norm2": 1.0 + nrm(ks[16], (DEPTH, D_MODEL), 0.02),
        "w_gate": nrm(ks[17], (DEPTH, D_MODEL, D_FF), D_MODEL ** -0.5),
        "w_up": nrm(ks[18], (DEPTH, D_MODEL, D_FF), D_MODEL ** -0.5),
        "w_down": nrm(ks[19], (DEPTH, D_FF, D_MODEL), D_FF ** -0.5),
        "g_final": 1.0 + nrm(ks[20], (D_MODEL,), 0.02),
    }


def reference(x_prompt, x_sample, c_prompt, c_sample, w_ada, b_ada, g_norm1, w_in, b_in,
              g_q, g_k, w_dw, b_dw, g_cn, b_cn, w_out, g_norm2, w_gate, w_up, w_down, g_final):
    def run(x, c):
        tabs = axial_rope_tables(x.shape[1])
        for l in range(DEPTH):
            x = encoder_layer(x, c, tabs, w_ada[l], b_ada[l], g_norm1[l], w_in[l], b_in[l],
                              g_q[l], g_k[l], w_dw[l], b_dw[l], g_cn[l], b_cn[l], w_out[l],
                              g_norm2[l], w_gate[l], w_up[l], w_down[l])
        return rms_norm(x, g_final)

    y_prompt = run(x_prompt, c_prompt)
    y_sample = run(x_sample, c_sample)
    return (y_prompt, y_sample)
```

```python
import functools
import math

import jax
import jax.numpy as jnp
from jax import lax
from jax.experimental import pallas as pl
from jax.experimental.pallas import tpu as pltpu

F32 = jnp.float32
BF16 = jnp.bfloat16

D_MODEL = 2048
HEAD_DIM = 128
N_HEADS = 8
N_KV_HEADS = 2
GROUP = N_HEADS // N_KV_HEADS
ATTN_DIM = N_HEADS * HEAD_DIM
KV_DIM = N_KV_HEADS * HEAD_DIM
CONV_DIM = D_MODEL - ATTN_DIM
CONV_WIDTH = 31
CONV_PAD = CONV_WIDTH // 2
IN_DIM = ATTN_DIM + 2 * KV_DIM + 2 * CONV_DIM
D_FF = 5632
GRID_W = 64
ROPE_THETA = 10000.0
EPS = 1e-6

LANES = 128
V7X_VMEM_LIMIT_BYTES = 56 * 1024 * 1024

ROW_TILE = 512
KV_CHUNK = 512
Q_TILE = 256
FF_TILE = 512
HALO = 16
CONV_ROWS = 256
MOD_COLS = 1536

Q_SCALE = math.log2(math.e) / math.sqrt(HEAD_DIM)


def _dot(a, b):
    return jnp.dot(a, b, preferred_element_type=F32)


def _params(*sem):
    return pltpu.CompilerParams(dimension_semantics=sem, vmem_limit_bytes=V7X_VMEM_LIMIT_BYTES)


def _const_spec(shape):
    nd = len(shape)
    return pl.BlockSpec(shape, lambda *_: (0,) * nd, pipeline_mode=pl.Buffered(1))


def _mod_kernel(c_ref, w_ref, b_ref, o_ref):
    c = c_ref[...]
    a = (c * jax.nn.sigmoid(c)).astype(BF16)
    o_ref[...] = _dot(a, w_ref[...].astype(BF16)) + b_ref[...]


def _modulation(c_pad, w_ada, b_ada):
    rows = c_pad.shape[0]
    n = w_ada.shape[1]
    return pl.pallas_call(
        _mod_kernel,
        grid=(n // MOD_COLS,),
        in_specs=[
            pl.BlockSpec((rows, D_MODEL), lambda j: (0, 0)),
            pl.BlockSpec((D_MODEL, MOD_COLS), lambda j: (0, j)),
            pl.BlockSpec((1, MOD_COLS), lambda j: (0, j)),
        ],
        out_specs=pl.BlockSpec((rows, MOD_COLS), lambda j: (0, j)),
        out_shape=jax.ShapeDtypeStruct((rows, n), F32),
        compiler_params=_params("arbitrary"),
        name="modulation",
    )(c_pad, w_ada, b_ada)


def _rope_kernel(cos_ref, sin_ref, *, n_rows):
    shape = cos_ref.shape
    r = lax.broadcasted_iota(jnp.int32, shape, 0)
    lane = lax.broadcasted_iota(jnp.int32, shape, 1)
    pos = jnp.where(r < n_rows, r, r - n_rows).astype(F32)
    freq = (lane % (HEAD_DIM // 4)).astype(F32)
    inv = jnp.exp(freq * (-2.0 / (HEAD_DIM // 2) * math.log(ROPE_THETA)))
    ang = pos * inv
    cos_ref[...] = jnp.cos(ang)
    sin_ref[...] = jnp.sin(ang)


def _rope_tables(seq):
    n_rows = seq // GRID_W
    n = n_rows + GRID_W
    cos, sin = pl.pallas_call(
        functools.partial(_rope_kernel, n_rows=n_rows),
        out_shape=(jax.ShapeDtypeStruct((n, LANES), F32), jax.ShapeDtypeStruct((n, LANES), F32)),
        name="rope_table",
    )()
    lane = jnp.arange(LANES)
    row_lane = ((lane // (HEAD_DIM // 4)) % 2 == 0)[None, :]
    sign = jnp.where(lane < HEAD_DIM // 2, -1.0, 1.0).astype(F32)[None, :]

    def expand(t):
        by_row = jnp.repeat(t[:n_rows], GRID_W, axis=0)
        by_col = jnp.tile(t[n_rows:], (n_rows, 1))
        return jnp.where(row_lane, by_row, by_col)

    return expand(cos), expand(sin) * sign


def _inproj_kernel(x_ref, mod_ref, g1_ref, w_ref, b_ref, gq_ref, gk_ref, cos_ref, sin_ref,
                   qt_ref, k_ref, vt_ref, u_ref):
    x = x_ref[0]
    ms = jnp.mean(x * x, axis=-1, keepdims=True)
    sh1 = mod_ref[0, 0:1, :]
    sc1 = mod_ref[0, 1:2, :]
    h = (x * lax.rsqrt(ms + EPS) * g1_ref[...]) * (1.0 + sc1) + sh1
    hb = h.astype(BF16)
    cos = cos_ref[...]
    sin = sin_ref[...]

    def norm_rope(z, g):
        r = lax.rsqrt(jnp.mean(z * z, axis=-1, keepdims=True) + EPS)
        y = z * r * g
        return y * cos + pltpu.roll(y, HEAD_DIM // 2, 1) * sin

    zq = _dot(hb, w_ref[:, 0:ATTN_DIM]) + b_ref[:, 0:ATTN_DIM]
    for hd in range(N_HEADS):
        sl = slice(hd * HEAD_DIM, (hd + 1) * HEAD_DIM)
        y = norm_rope(zq[:, sl], gq_ref[...]) * Q_SCALE
        qt_ref[0, sl, :] = y.T.astype(BF16)

    k0 = ATTN_DIM
    zk = _dot(hb, w_ref[:, k0:k0 + KV_DIM]) + b_ref[:, k0:k0 + KV_DIM]
    for hd in range(N_KV_HEADS):
        sl = slice(hd * HEAD_DIM, (hd + 1) * HEAD_DIM)
        k_ref[0, :, sl] = norm_rope(zk[:, sl], gk_ref[...]).astype(BF16)

    v0 = ATTN_DIM + KV_DIM
    zv = _dot(hb, w_ref[:, v0:v0 + KV_DIM]) + b_ref[:, v0:v0 + KV_DIM]
    vt_ref[0, 0] = zv.T.astype(BF16)

    u0 = ATTN_DIM + 2 * KV_DIM
    za = _dot(hb, w_ref[:, u0:u0 + CONV_DIM]) + b_ref[:, u0:u0 + CONV_DIM]
    zg = _dot(hb, w_ref[:, u0 + CONV_DIM:]) + b_ref[:, u0 + CONV_DIM:]
    u_ref[0] = za * jax.nn.sigmoid(zg)


def _in_proj(x, mod, g1, w_in, b_in, gq, gk, cos, sin):
    bsz, seq, _ = x.shape
    nt = seq // ROW_TILE
    return pl.pallas_call(
        _inproj_kernel,
        grid=(bsz, nt),
        in_specs=[
            pl.BlockSpec((1, ROW_TILE, D_MODEL), lambda b, i: (b, i, 0)),
            pl.BlockSpec((1, 6, D_MODEL), lambda b, i: (b, 0, 0)),
            _const_spec((1, D_MODEL)),
            _const_spec((D_MODEL, IN_DIM)),
            _const_spec((1, IN_DIM)),
            _const_spec((1, HEAD_DIM)),
            _const_spec((1, HEAD_DIM)),
            pl.BlockSpec((ROW_TILE, LANES), lambda b, i: (i, 0)),
            pl.BlockSpec((ROW_TILE, LANES), lambda b, i: (i, 0)),
        ],
        out_specs=[
            pl.BlockSpec((1, ATTN_DIM, ROW_TILE), lambda b, i: (b, 0, i)),
            pl.BlockSpec((1, ROW_TILE, KV_DIM), lambda b, i: (b, i, 0)),
            pl.BlockSpec((1, 1, KV_DIM, KV_CHUNK), lambda b, i: (b, i, 0, 0)),
            pl.BlockSpec((1, ROW_TILE, CONV_DIM), lambda b, i: (b, i, 0)),
        ],
        out_shape=[
            jax.ShapeDtypeStruct((bsz, ATTN_DIM, seq), BF16),
            jax.ShapeDtypeStruct((bsz, seq, KV_DIM), BF16),
            jax.ShapeDtypeStruct((bsz, nt, KV_DIM, KV_CHUNK), BF16),
            jax.ShapeDtypeStruct((bsz, seq, CONV_DIM), F32),
        ],
        compiler_params=_params("parallel", "parallel"),
        name="in_proj",
    )(x, mod, g1, w_in, b_in, gq, gk, cos, sin)


def _attn_kernel(qt_ref, k_ref, vt_ref, o_ref, m_ref, l_ref, acc_ref, *, n_chunks):
    m_ref[...] = jnp.full(m_ref.shape, -jnp.inf, F32)
    l_ref[...] = jnp.zeros(l_ref.shape, F32)
    acc_ref[...] = jnp.zeros(acc_ref.shape, F32)

    def body(c, carry):
        off = pl.multiple_of(c * KV_CHUNK, KV_CHUNK)
        kb = k_ref[0, pl.ds(off, KV_CHUNK), :]
        vb = vt_ref[0, c]
        for g in range(GROUP):
            qg = qt_ref[0, g * HEAD_DIM:(g + 1) * HEAD_DIM, :]
            s = _dot(kb, qg)
            m_old = m_ref[g]
            m_new = jnp.maximum(m_old, jnp.max(s, axis=0, keepdims=True))
            alpha = jnp.exp2(m_old - m_new)
            p = jnp.exp2(s - m_new)
            l_ref[g] = alpha * l_ref[g] + jnp.sum(p, axis=0, keepdims=True)
            acc_ref[g] = alpha * acc_ref[g] + _dot(vb, p.astype(BF16))
            m_ref[g] = m_new
        return carry

    lax.fori_loop(0, n_chunks, body, 0)

    for g in range(GROUP):
        o = acc_ref[g] * (1.0 / l_ref[g])
        o_ref[0, :, g * HEAD_DIM:(g + 1) * HEAD_DIM] = o.T.astype(BF16)


def _attention(qt, k, vt):
    bsz, _, seq = qt.shape
    n_chunks = seq // KV_CHUNK
    gw = GROUP * HEAD_DIM
    return pl.pallas_call(
        functools.partial(_attn_kernel, n_chunks=n_chunks),
        grid=(bsz, N_KV_HEADS, seq // Q_TILE),
        in_specs=[
            pl.BlockSpec((1, gw, Q_TILE), lambda b, h, i: (b, h, i)),
            pl.BlockSpec((1, seq, HEAD_DIM), lambda b, h, i: (b, 0, h)),
            pl.BlockSpec((1, n_chunks, HEAD_DIM, KV_CHUNK), lambda b, h, i: (b, 0, h, 0)),
        ],
        out_specs=pl.BlockSpec((1, Q_TILE, gw), lambda b, h, i: (b, i, h)),
        out_shape=jax.ShapeDtypeStruct((bsz, seq, ATTN_DIM), BF16),
        scratch_shapes=[
            pltpu.VMEM((GROUP, 1, Q_TILE), F32),
            pltpu.VMEM((GROUP, 1, Q_TILE), F32),
            pltpu.VMEM((GROUP, HEAD_DIM, Q_TILE), F32),
        ],
        compiler_params=_params("parallel", "parallel", "arbitrary"),
        name="attention",
    )(qt, k, vt)


def _outproj_kernel(attn_ref, u_ref, up_ref, un_ref, x_ref, mod_ref, wdw_ref, bdw_ref, gcn_ref, bcn_ref,
                    wout_ref, g2_ref, x1_ref, h2_ref, buf_ref, conv_ref):
    i = pl.program_id(1)
    last = pl.num_programs(1) - 1
    buf_ref[0:HALO, :] = jnp.where(i > 0, up_ref[0], 0.0)
    buf_ref[HALO:HALO + ROW_TILE, :] = u_ref[0]
    buf_ref[HALO + ROW_TILE:, :] = jnp.where(i < last, un_ref[0], 0.0)

    base = HALO - CONV_PAD
    for r0 in range(0, ROW_TILE, CONV_ROWS):
        for c0 in range(0, CONV_DIM, LANES):
            cs = slice(c0, c0 + LANES)
            acc = jnp.broadcast_to(bdw_ref[:, cs], (CONV_ROWS, LANES))
            for j in range(CONV_WIDTH):
                lo = base + r0 + j
                acc = acc + buf_ref[lo:lo + CONV_ROWS, cs] * wdw_ref[j:j + 1, cs]
            conv_ref[r0:r0 + CONV_ROWS, cs] = acc

    hc = conv_ref[...]
    mu = jnp.mean(hc, axis=-1, keepdims=True)
    d = hc - mu
    var = jnp.mean(d * d, axis=-1, keepdims=True)
    y = d * lax.rsqrt(var + EPS) * gcn_ref[...] + bcn_ref[...]
    conv = (y * jax.nn.sigmoid(y)).astype(BF16)

    mix = _dot(attn_ref[0], wout_ref[0:ATTN_DIM, :]) + _dot(conv, wout_ref[ATTN_DIM:, :])
    gt1 = mod_ref[0, 2:3, :]
    x1 = x_ref[0] + gt1 * mix
    x1_ref[0] = x1

    sh2 = mod_ref[0, 3:4, :]
    sc2 = mod_ref[0, 4:5, :]
    ms = jnp.mean(x1 * x1, axis=-1, keepdims=True)
    h2 = (x1 * lax.rsqrt(ms + EPS) * g2_ref[...]) * (1.0 + sc2) + sh2
    h2_ref[0] = h2.astype(BF16)


def _out_proj(attn, u, x, mod, w_dw, b_dw, g_cn, b_cn, w_out, g2):
    bsz, seq, _ = x.shape
    nt = seq // ROW_TILE
    hb = ROW_TILE // HALO
    n_halo = seq // HALO
    return pl.pallas_call(
        _outproj_kernel,
        grid=(bsz, nt),
        in_specs=[
            pl.BlockSpec((1, ROW_TILE, ATTN_DIM), lambda b, i: (b, i, 0)),
            pl.BlockSpec((1, ROW_TILE, CONV_DIM), lambda b, i: (b, i, 0)),
            pl.BlockSpec((1, HALO, CONV_DIM), lambda b, i: (b, jnp.maximum(i * hb - 1, 0), 0)),
            pl.BlockSpec((1, HALO, CONV_DIM), lambda b, i: (b, jnp.minimum((i + 1) * hb, n_halo - 1), 0)),
            pl.BlockSpec((1, ROW_TILE, D_MODEL), lambda b, i: (b, i, 0)),
            pl.BlockSpec((1, 6, D_MODEL), lambda b, i: (b, 0, 0)),
            _const_spec((CONV_WIDTH, CONV_DIM)),
            _const_spec((1, CONV_DIM)),
            _const_spec((1, CONV_DIM)),
            _const_spec((1, CONV_DIM)),
            _const_spec((D_MODEL, D_MODEL)),
            _const_spec((1, D_MODEL)),
        ],
        out_specs=[
            pl.BlockSpec((1, ROW_TILE, D_MODEL), lambda b, i: (b, i, 0)),
            pl.BlockSpec((1, ROW_TILE, D_MODEL), lambda b, i: (b, i, 0)),
        ],
        out_shape=[
            jax.ShapeDtypeStruct((bsz, seq, D_MODEL), F32),
            jax.ShapeDtypeStruct((bsz, seq, D_MODEL), BF16),
        ],
        scratch_shapes=[
            pltpu.VMEM((ROW_TILE + 2 * HALO, CONV_DIM), F32),
            pltpu.VMEM((ROW_TILE, CONV_DIM), F32),
        ],
        compiler_params=_params("parallel", "parallel"),
        name="out_proj",
    )(attn, u, u, u, x, mod, w_dw, b_dw, g_cn, b_cn, w_out, g2)


def _ffn_kernel(h2_ref, x1_ref, mod_ref, wg_ref, wu_ref, wd_ref, gf_ref, y_ref, acc_ref):
    j = pl.program_id(2)

    @pl.when(j == 0)
    def _():
        acc_ref[...] = jnp.zeros(acc_ref.shape, F32)

    h = h2_ref[0]
    g = _dot(h, wg_ref[...])
    u = _dot(h, wu_ref[...])
    a = (g * jax.nn.sigmoid(g) * u).astype(BF16)
    acc_ref[...] += _dot(a, wd_ref[...])

    @pl.when(j == pl.num_programs(2) - 1)
    def _():
        gt2 = mod_ref[0, 5:6, :]
        x2 = x1_ref[0] + gt2 * acc_ref[...]
        ms = jnp.mean(x2 * x2, axis=-1, keepdims=True)
        y_ref[0] = x2 * lax.rsqrt(ms + EPS) * gf_ref[...]


def _ffn(h2, x1, mod, w_gate, w_up, w_down, g_final):
    bsz, seq, _ = x1.shape
    return pl.pallas_call(
        _ffn_kernel,
        grid=(bsz, seq // ROW_TILE, D_FF // FF_TILE),
        in_specs=[
            pl.BlockSpec((1, ROW_TILE, D_MODEL), lambda b, i, j: (b, i, 0)),
            pl.BlockSpec((1, ROW_TILE, D_MODEL), lambda b, i, j: (b, i, 0)),
            pl.BlockSpec((1, 6, D_MODEL), lambda b, i, j: (b, 0, 0)),
            pl.BlockSpec((D_MODEL, FF_TILE), lambda b, i, j: (0, j)),
            pl.BlockSpec((D_MODEL, FF_TILE), lambda b, i, j: (0, j)),
            pl.BlockSpec((FF_TILE, D_MODEL), lambda b, i, j: (j, 0)),
            _const_spec((1, D_MODEL)),
        ],
        out_specs=pl.BlockSpec((1, ROW_TILE, D_MODEL), lambda b, i, j: (b, i, 0)),
        out_shape=jax.ShapeDtypeStruct((bsz, seq, D_MODEL), F32),
        scratch_shapes=[pltpu.VMEM((ROW_TILE, D_MODEL), F32)],
        compiler_params=_params("parallel", "parallel", "arbitrary"),
        name="ffn",
    )(h2, x1, mod, w_gate, w_up, w_down, g_final)


def _reorder_head_dims(a):
    lead = a.shape[:-1]
    a = a.reshape(*lead, -1, 2, 2, HEAD_DIM // 4)
    return jnp.swapaxes(a, -3, -2).reshape(*lead, -1)


def _reorder_qk_columns(a):
    n_qk = (N_HEADS + N_KV_HEADS) * HEAD_DIM
    return jnp.concatenate([_reorder_head_dims(a[..., :n_qk]), a[..., n_qk:]], axis=-1)


def _layer(x, mod, tabs, p):
    cos, sin = tabs
    qt, k, vt, u = _in_proj(x, mod, p["g1"], p["w_in"], p["b_in"], p["gq"], p["gk"], cos, sin)
    attn = _attention(qt, k, vt)
    x1, h2 = _out_proj(attn, u, x, mod, p["w_dw"], p["b_dw"], p["g_cn"], p["b_cn"], p["w_out"], p["g2"])
    return _ffn(h2, x1, mod, p["w_gate"], p["w_up"], p["w_down"], p["g_final"])


def kernel(x_prompt, x_sample, c_prompt, c_sample, w_ada, b_ada, g_norm1, w_in, b_in, g_q, g_k, w_dw, b_dw,
           g_cn, b_cn, w_out, g_norm2, w_gate, w_up, w_down, g_final):
    assert w_ada.shape[0] == 1, "single layer"
    assert x_prompt.shape[1] == x_sample.shape[1]
    seq = x_prompt.shape[1]
    assert seq % ROW_TILE == 0 and seq % Q_TILE == 0 and seq % GRID_W == 0
    nb_p, nb_s = x_prompt.shape[0], x_sample.shape[0]

    c_all = jnp.concatenate([c_prompt, c_sample], axis=0)
    pad = (-c_all.shape[0]) % 8
    c_pad = jnp.pad(c_all, ((0, pad), (0, 0)))
    mod = _modulation(c_pad, w_ada[0], b_ada[0][None, :]).reshape(c_pad.shape[0], 6, D_MODEL)

    p = {
        "g1": g_norm1[0][None, :],
        "w_in": _reorder_qk_columns(w_in[0].astype(BF16)),
        "b_in": _reorder_qk_columns(b_in[0])[None, :],
        "gq": _reorder_head_dims(g_q[0])[None, :],
        "gk": _reorder_head_dims(g_k[0])[None, :],
        "w_dw": w_dw[0],
        "b_dw": b_dw[0][None, :],
        "g_cn": g_cn[0][None, :],
        "b_cn": b_cn[0][None, :],
        "w_out": w_out[0].astype(BF16),
        "g2": g_norm2[0][None, :],
        "w_gate": w_gate[0].astype(BF16),
        "w_up": w_up[0].astype(BF16),
        "w_down": w_down[0].astype(BF16),
        "g_final": g_final[None, :],
    }
    tabs = _rope_tables(seq)
    y_prompt = _layer(x_prompt, mod[:nb_p], tabs, p)
    y_sample = _layer(x_sample, mod[nb_p:nb_p + nb_s], tabs, p)
    return (y_prompt, y_sample)
```

```python
import functools
import math

import jax
import jax.numpy as jnp
from jax import lax
from jax.experimental import pallas as pl
from jax.experimental.pallas import tpu as pltpu

F32 = jnp.float32
BF16 = jnp.bfloat16

D_MODEL = 2048
HEAD_DIM = 128
N_HEADS = 8
N_KV_HEADS = 2
GROUP = N_HEADS // N_KV_HEADS
ATTN_DIM = N_HEADS * HEAD_DIM
KV_DIM = N_KV_HEADS * HEAD_DIM
CONV_DIM = D_MODEL - ATTN_DIM
CONV_WIDTH = 31
CONV_PAD = CONV_WIDTH // 2
IN_DIM = ATTN_DIM + 2 * KV_DIM + 2 * CONV_DIM
D_FF = 5632
GRID_W = 64
ROPE_THETA = 10000.0
EPS = 1e-6

LANES = 128
SUBLANES = 8
BF16_ROWS = 16
MXU_DEPTH = 256
V7X_VMEM_LIMIT_BYTES = 56 * 1024 * 1024

ROW_TILE = 512
KV_CHUNK = 512
Q_TILE = 256
FF_TILE = 512
HALO = 16
CONV_ROWS = 256
MOD_COLS = 1536

Q_SCALE = math.log2(math.e) / math.sqrt(HEAD_DIM)
SAFE_SHIFT = 48.0
NORM_MARGIN = 1.02


def _dot(a, b):
    return jnp.dot(a, b, preferred_element_type=F32)


def _params(*sem):
    return pltpu.CompilerParams(dimension_semantics=sem, vmem_limit_bytes=V7X_VMEM_LIMIT_BYTES)


def _const_spec(shape):
    nd = len(shape)
    return pl.BlockSpec(shape, lambda *_: (0,) * nd, pipeline_mode=pl.Buffered(1))


def _mod_kernel(c_ref, w_ref, b_ref, o_ref):
    c = c_ref[...]
    a = (c * jax.nn.sigmoid(c)).astype(BF16)
    o_ref[...] = _dot(a, w_ref[...].astype(BF16)) + b_ref[...]


def _modulation(c_pad, w_ada, b_ada):
    rows = c_pad.shape[0]
    n = w_ada.shape[1]
    return pl.pallas_call(
        _mod_kernel,
        grid=(n // MOD_COLS,),
        in_specs=[
            pl.BlockSpec((rows, D_MODEL), lambda j: (0, 0)),
            pl.BlockSpec((D_MODEL, MOD_COLS), lambda j: (0, j)),
            pl.BlockSpec((1, MOD_COLS), lambda j: (0, j)),
        ],
        out_specs=pl.BlockSpec((rows, MOD_COLS), lambda j: (0, j)),
        out_shape=jax.ShapeDtypeStruct((rows, n), F32),
        compiler_params=_params("arbitrary"),
        name="modulation",
    )(c_pad, w_ada, b_ada)


def _rope_kernel(cos_ref, sin_ref, *, n_rows):
    shape = cos_ref.shape
    r = lax.broadcasted_iota(jnp.int32, shape, 0)
    lane = lax.broadcasted_iota(jnp.int32, shape, 1)
    pos = jnp.where(r < n_rows, r, r - n_rows).astype(F32)
    freq = (lane % (HEAD_DIM // 4)).astype(F32)
    inv = jnp.exp(freq * (-2.0 / (HEAD_DIM // 2) * math.log(ROPE_THETA)))
    ang = pos * inv
    cos_ref[...] = jnp.cos(ang)
    sin_ref[...] = jnp.sin(ang)


def _rope_tables(seq):
    n_rows = seq // GRID_W
    n = n_rows + GRID_W
    cos, sin = pl.pallas_call(
        functools.partial(_rope_kernel, n_rows=n_rows),
        out_shape=(jax.ShapeDtypeStruct((n, LANES), F32), jax.ShapeDtypeStruct((n, LANES), F32)),
        name="rope_table",
    )()
    lane = jnp.arange(LANES)
    row_lane = ((lane // (HEAD_DIM // 4)) % 2 == 0)[None, :]
    sign = jnp.where(lane < HEAD_DIM // 2, -1.0, 1.0).astype(F32)[None, :]

    def expand(t):
        by_row = jnp.repeat(t[:n_rows], GRID_W, axis=0)
        by_col = jnp.tile(t[n_rows:], (n_rows, 1))
        return jnp.where(row_lane, by_row, by_col)

    return expand(cos), expand(sin) * sign


def _inproj_kernel(x_ref, mod_ref, g1_ref, w_ref, b_ref, gq_ref, gk_ref, cos_ref, sin_ref,
                   qt_ref, k_ref, vt_ref, u_ref, kn_ref):
    x = x_ref[0]
    ms = jnp.mean(x * x, axis=-1, keepdims=True)
    sh1 = mod_ref[0, 0:1, :]
    sc1 = mod_ref[0, 1:2, :]
    h = (x * lax.rsqrt(ms + EPS) * g1_ref[...]) * (1.0 + sc1) + sh1
    hb = h.astype(BF16)
    cos = cos_ref[...]
    sin = sin_ref[...]

    def norm_rope(z, g):
        r = lax.rsqrt(jnp.mean(z * z, axis=-1, keepdims=True) + EPS)
        y = z * r * g
        return y * cos + pltpu.roll(y, HEAD_DIM // 2, 1) * sin

    zq = _dot(hb, w_ref[:, 0:ATTN_DIM]) + b_ref[:, 0:ATTN_DIM]
    for hd in range(N_HEADS):
        sl = slice(hd * HEAD_DIM, (hd + 1) * HEAD_DIM)
        y = norm_rope(zq[:, sl], gq_ref[...]) * Q_SCALE
        qt_ref[0, sl, :] = y.T.astype(BF16)

    @pl.when(pl.program_id(1) == 0)
    def _():
        kn_ref[...] = jnp.zeros(kn_ref.shape, F32)

    k0 = ATTN_DIM
    zk = _dot(hb, w_ref[:, k0:k0 + KV_DIM]) + b_ref[:, k0:k0 + KV_DIM]
    for hd in range(N_KV_HEADS):
        sl = slice(hd * HEAD_DIM, (hd + 1) * HEAD_DIM)
        kh = norm_rope(zk[:, sl], gk_ref[...])
        k_ref[0, :, sl] = kh.astype(BF16)
        n2 = jnp.max(jnp.sum(kh * kh, axis=-1, keepdims=True), axis=0, keepdims=True)
        kn_ref[0, hd:hd + 1, :] = jnp.maximum(kn_ref[0, hd:hd + 1, :], n2)

    v0 = ATTN_DIM + KV_DIM
    zv = _dot(hb, w_ref[:, v0:v0 + KV_DIM]) + b_ref[:, v0:v0 + KV_DIM]
    vt_ref[0, 0] = zv.T.astype(BF16)

    u0 = ATTN_DIM + 2 * KV_DIM
    za = _dot(hb, w_ref[:, u0:u0 + CONV_DIM]) + b_ref[:, u0:u0 + CONV_DIM]
    zg = _dot(hb, w_ref[:, u0 + CONV_DIM:]) + b_ref[:, u0 + CONV_DIM:]
    u_ref[0] = za * jax.nn.sigmoid(zg)


def _in_proj(x, mod, g1, w_in, b_in, gq, gk, cos, sin):
    bsz, seq, _ = x.shape
    nt = seq // ROW_TILE
    return pl.pallas_call(
        _inproj_kernel,
        grid=(bsz, nt),
        in_specs=[
            pl.BlockSpec((1, ROW_TILE, D_MODEL), lambda b, i: (b, i, 0)),
            pl.BlockSpec((1, 6, D_MODEL), lambda b, i: (b, 0, 0)),
            _const_spec((1, D_MODEL)),
            _const_spec((D_MODEL, IN_DIM)),
            _const_spec((1, IN_DIM)),
            _const_spec((1, HEAD_DIM)),
            _const_spec((1, HEAD_DIM)),
            pl.BlockSpec((ROW_TILE, LANES), lambda b, i: (i, 0)),
            pl.BlockSpec((ROW_TILE, LANES), lambda b, i: (i, 0)),
        ],
        out_specs=[
            pl.BlockSpec((1, ATTN_DIM, ROW_TILE), lambda b, i: (b, 0, i)),
            pl.BlockSpec((1, ROW_TILE, KV_DIM), lambda b, i: (b, i, 0)),
            pl.BlockSpec((1, 1, KV_DIM, KV_CHUNK), lambda b, i: (b, i, 0, 0)),
            pl.BlockSpec((1, ROW_TILE, CONV_DIM), lambda b, i: (b, i, 0)),
            pl.BlockSpec((1, SUBLANES, LANES), lambda b, i: (b, 0, 0)),
        ],
        out_shape=[
            jax.ShapeDtypeStruct((bsz, ATTN_DIM, seq), BF16),
            jax.ShapeDtypeStruct((bsz, seq, KV_DIM), BF16),
            jax.ShapeDtypeStruct((bsz, nt, KV_DIM, KV_CHUNK), BF16),
            jax.ShapeDtypeStruct((bsz, seq, CONV_DIM), F32),
            jax.ShapeDtypeStruct((bsz, SUBLANES, LANES), F32),
        ],
        compiler_params=_params("parallel", "arbitrary"),
        name="in_proj",
    )(x, mod, g1, w_in, b_in, gq, gk, cos, sin)


def _attn_kernel(qt_ref, k_ref, vt_ref, kn_ref, o_ref, qa_ref, p0_ref, p1_ref, m_ref, l_ref, acc_ref, *, n_chunks):
    h = pl.program_id(1)
    tq = qt_ref.shape[2]
    gw = GROUP * tq
    k_norm = jnp.sqrt(kn_ref[0, pl.ds(h, 1), :][:, 0:1])
    row = lax.broadcasted_iota(jnp.int32, (BF16_ROWS, tq), 0)
    worst = jnp.zeros((1, 1), F32)
    for g in range(GROUP):
        q = qt_ref[0, g * HEAD_DIM:(g + 1) * HEAD_DIM, :]
        qf = q.astype(F32)
        shift = jnp.sqrt(jnp.sum(qf * qf, axis=0, keepdims=True)) * k_norm * NORM_MARGIN
        worst = jnp.maximum(worst, jnp.max(shift, axis=1, keepdims=True))
        cs = slice(g * tq, (g + 1) * tq)
        qa_ref[0:HEAD_DIM, cs] = q
        qa_ref[HEAD_DIM:HEAD_DIM + BF16_ROWS, cs] = jnp.where(row == 0, -shift, 0.0).astype(BF16)
        qa_ref[HEAD_DIM + BF16_ROWS:, cs] = jnp.zeros((MXU_DEPTH - HEAD_DIM - BF16_ROWS, tq), BF16)
    safe = worst[0, 0] <= SAFE_SHIFT

    acc_ref[...] = jnp.zeros(acc_ref.shape, F32)

    @pl.when(safe)
    def _():
        lane = lax.broadcasted_iota(jnp.int32, (KV_CHUNK, LANES), 1)
        ones_col = jnp.where(lane == 0, 1.0, 0.0).astype(BF16)

        def qk_exp(c, p_ref):
            off = pl.multiple_of(c * KV_CHUNK, KV_CHUNK)
            ka = jnp.concatenate([k_ref[0, pl.ds(off, KV_CHUNK), :], ones_col], axis=1)
            p = jnp.exp2(_dot(ka, qa_ref[...]))
            p_ref[...] = p.astype(BF16)
            return jnp.sum(p.reshape(KV_CHUNK // SUBLANES, SUBLANES, gw), axis=0)

        def pv(c, p_ref):
            acc_ref[...] += _dot(vt_ref[0, c], p_ref[...])

        def body(i, l):
            c = 2 * i
            l = l + qk_exp(c + 1, p1_ref)
            pv(c, p0_ref)
            l = l + qk_exp(c + 2, p0_ref)
            pv(c + 1, p1_ref)
            return l

        l = lax.fori_loop(0, n_chunks // 2 - 1, body, qk_exp(0, p0_ref))
        l = l + qk_exp(n_chunks - 1, p1_ref)
        pv(n_chunks - 2, p0_ref)
        pv(n_chunks - 1, p1_ref)
        l_ref[...] = l

    @pl.when(jnp.logical_not(safe))
    def _():
        m_ref[...] = jnp.full(m_ref.shape, -jnp.inf, F32)
        l_ref[...] = jnp.zeros(l_ref.shape, F32)
        first_row = lax.broadcasted_iota(jnp.int32, (SUBLANES, tq), 0) == 0

        def body(c, carry):
            off = pl.multiple_of(c * KV_CHUNK, KV_CHUNK)
            kb = k_ref[0, pl.ds(off, KV_CHUNK), :]
            vb = vt_ref[0, c]
            for g in range(GROUP):
                cs = slice(g * tq, (g + 1) * tq)
                s = _dot(kb, qt_ref[0, g * HEAD_DIM:(g + 1) * HEAD_DIM, :])
                m_old = m_ref[:, cs]
                m_new = jnp.maximum(m_old, jnp.max(s, axis=0, keepdims=True))
                alpha = jnp.exp2(m_old - m_new)
                p = jnp.exp2(s - m_new)
                l_ref[:, cs] = alpha * l_ref[:, cs] + jnp.where(first_row, jnp.sum(p, axis=0, keepdims=True), 0.0)
                acc_ref[:, cs] = alpha * acc_ref[:, cs] + _dot(vb, p.astype(BF16))
                m_ref[:, cs] = m_new
            return carry

        lax.fori_loop(0, n_chunks, body, 0)

    o = acc_ref[...] * (1.0 / jnp.sum(l_ref[...], axis=0, keepdims=True))
    for g in range(GROUP):
        o_ref[0, :, g * HEAD_DIM:(g + 1) * HEAD_DIM] = o[:, g * tq:(g + 1) * tq].T.astype(BF16)


def _attention(qt, k, vt, kn):
    bsz, _, seq = qt.shape
    n_chunks = seq // KV_CHUNK
    assert n_chunks % 2 == 0 and n_chunks >= 4
    gw = GROUP * HEAD_DIM
    return pl.pallas_call(
        functools.partial(_attn_kernel, n_chunks=n_chunks),
        grid=(bsz, N_KV_HEADS, seq // Q_TILE),
        in_specs=[
            pl.BlockSpec((1, gw, Q_TILE), lambda b, h, i: (b, h, i)),
            pl.BlockSpec((1, seq, HEAD_DIM), lambda b, h, i: (b, 0, h)),
            pl.BlockSpec((1, n_chunks, HEAD_DIM, KV_CHUNK), lambda b, h, i: (b, 0, h, 0)),
            pl.BlockSpec((1, SUBLANES, LANES), lambda b, h, i: (b, 0, 0)),
        ],
        out_specs=pl.BlockSpec((1, Q_TILE, gw), lambda b, h, i: (b, i, h)),
        out_shape=jax.ShapeDtypeStruct((bsz, seq, ATTN_DIM), BF16),
        scratch_shapes=[
            pltpu.VMEM((MXU_DEPTH, GROUP * Q_TILE), BF16),
            pltpu.VMEM((KV_CHUNK, GROUP * Q_TILE), BF16),
            pltpu.VMEM((KV_CHUNK, GROUP * Q_TILE), BF16),
            pltpu.VMEM((1, GROUP * Q_TILE), F32),
            pltpu.VMEM((SUBLANES, GROUP * Q_TILE), F32),
            pltpu.VMEM((HEAD_DIM, GROUP * Q_TILE), F32),
        ],
        compiler_params=_params("parallel", "parallel", "arbitrary"),
        name="attention",
    )(qt, k, vt, kn)


def _outproj_kernel(attn_ref, u_ref, up_ref, un_ref, x_ref, mod_ref, wdw_ref, bdw_ref, gcn_ref, bcn_ref,
                    wout_ref, g2_ref, x1_ref, h2_ref, buf_ref, conv_ref):
    i = pl.program_id(1)
    last = pl.num_programs(1) - 1
    buf_ref[0:HALO, :] = jnp.where(i > 0, up_ref[0], 0.0)
    buf_ref[HALO:HALO + ROW_TILE, :] = u_ref[0]
    buf_ref[HALO + ROW_TILE:, :] = jnp.where(i < last, un_ref[0], 0.0)

    base = HALO - CONV_PAD
    for r0 in range(0, ROW_TILE, CONV_ROWS):
        for c0 in range(0, CONV_DIM, LANES):
            cs = slice(c0, c0 + LANES)
            acc = jnp.broadcast_to(bdw_ref[:, cs], (CONV_ROWS, LANES))
            for j in range(CONV_WIDTH):
                lo = base + r0 + j
                acc = acc + buf_ref[lo:lo + CONV_ROWS, cs] * wdw_ref[j:j + 1, cs]
            conv_ref[r0:r0 + CONV_ROWS, cs] = acc

    hc = conv_ref[...]
    mu = jnp.mean(hc, axis=-1, keepdims=True)
    d = hc - mu
    var = jnp.mean(d * d, axis=-1, keepdims=True)
    y = d * lax.rsqrt(var + EPS) * gcn_ref[...] + bcn_ref[...]
    conv = (y * jax.nn.sigmoid(y)).astype(BF16)

    mix = _dot(attn_ref[0], wout_ref[0:ATTN_DIM, :]) + _dot(conv, wout_ref[ATTN_DIM:, :])
    gt1 = mod_ref[0, 2:3, :]
    x1 = x_ref[0] + gt1 * mix
    x1_ref[0] = x1

    sh2 = mod_ref[0, 3:4, :]
    sc2 = mod_ref[0, 4:5, :]
    ms = jnp.mean(x1 * x1, axis=-1, keepdims=True)
    h2 = (x1 * lax.rsqrt(ms + EPS) * g2_ref[...]) * (1.0 + sc2) + sh2
    h2_ref[0] = h2.astype(BF16)


def _out_proj(attn, u, x, mod, w_dw, b_dw, g_cn, b_cn, w_out, g2):
    bsz, seq, _ = x.shape
    nt = seq // ROW_TILE
    hb = ROW_TILE // HALO
    n_halo = seq // HALO
    return pl.pallas_call(
        _outproj_kernel,
        grid=(bsz, nt),
        in_specs=[
            pl.BlockSpec((1, ROW_TILE, ATTN_DIM), lambda b, i: (b, i, 0)),
            pl.BlockSpec((1, ROW_TILE, CONV_DIM), lambda b, i: (b, i, 0)),
            pl.BlockSpec((1, HALO, CONV_DIM), lambda b, i: (b, jnp.maximum(i * hb - 1, 0), 0)),
            pl.BlockSpec((1, HALO, CONV_DIM), lambda b, i: (b, jnp.minimum((i + 1) * hb, n_halo - 1), 0)),
            pl.BlockSpec((1, ROW_TILE, D_MODEL), lambda b, i: (b, i, 0)),
            pl.BlockSpec((1, 6, D_MODEL), lambda b, i: (b, 0, 0)),
            _const_spec((CONV_WIDTH, CONV_DIM)),
            _const_spec((1, CONV_DIM)),
            _const_spec((1, CONV_DIM)),
            _const_spec((1, CONV_DIM)),
            _const_spec((D_MODEL, D_MODEL)),
            _const_spec((1, D_MODEL)),
        ],
        out_specs=[
            pl.BlockSpec((1, ROW_TILE, D_MODEL), lambda b, i: (b, i, 0)),
            pl.BlockSpec((1, ROW_TILE, D_MODEL), lambda b, i: (b, i, 0)),
        ],
        out_shape=[
            jax.ShapeDtypeStruct((bsz, seq, D_MODEL), F32),
            jax.ShapeDtypeStruct((bsz, seq, D_MODEL), BF16),
        ],
        scratch_shapes=[
            pltpu.VMEM((ROW_TILE + 2 * HALO, CONV_DIM), F32),
            pltpu.VMEM((ROW_TILE, CONV_DIM), F32),
        ],
        compiler_params=_params("parallel", "parallel"),
        name="out_proj",
    )(attn, u, u, u, x, mod, w_dw, b_dw, g_cn, b_cn, w_out, g2)


def _ffn_kernel(h2_ref, x1_ref, mod_ref, wg_ref, wu_ref, wd_ref, gf_ref, y_ref, acc_ref):
    j = pl.program_id(2)

    @pl.when(j == 0)
    def _():
        acc_ref[...] = jnp.zeros(acc_ref.shape, F32)

    h = h2_ref[0]
    g = _dot(h, wg_ref[...])
    u = _dot(h, wu_ref[...])
    a = (g * jax.nn.sigmoid(g) * u).astype(BF16)
    acc_ref[...] += _dot(a, wd_ref[...])

    @pl.when(j == pl.num_programs(2) - 1)
    def _():
        gt2 = mod_ref[0, 5:6, :]
        x2 = x1_ref[0] + gt2 * acc_ref[...]
        ms = jnp.mean(x2 * x2, axis=-1, keepdims=True)
        y_ref[0] = x2 * lax.rsqrt(ms + EPS) * gf_ref[...]


def _ffn(h2, x1, mod, w_gate, w_up, w_down, g_final):
    bsz, seq, _ = x1.shape
    return pl.pallas_call(
        _ffn_kernel,
        grid=(bsz, seq // ROW_TILE, D_FF // FF_TILE),
        in_specs=[
            pl.BlockSpec((1, ROW_TILE, D_MODEL), lambda b, i, j: (b, i, 0)),
            pl.BlockSpec((1, ROW_TILE, D_MODEL), lambda b, i, j: (b, i, 0)),
            pl.BlockSpec((1, 6, D_MODEL), lambda b, i, j: (b, 0, 0)),
            pl.BlockSpec((D_MODEL, FF_TILE), lambda b, i, j: (0, j)),
            pl.BlockSpec((D_MODEL, FF_TILE), lambda b, i, j: (0, j)),
            pl.BlockSpec((FF_TILE, D_MODEL), lambda b, i, j: (j, 0)),
            _const_spec((1, D_MODEL)),
        ],
        out_specs=pl.BlockSpec((1, ROW_TILE, D_MODEL), lambda b, i, j: (b, i, 0)),
        out_shape=jax.ShapeDtypeStruct((bsz, seq, D_MODEL), F32),
        scratch_shapes=[pltpu.VMEM((ROW_TILE, D_MODEL), F32)],
        compiler_params=_params("parallel", "parallel", "arbitrary"),
        name="ffn",
    )(h2, x1, mod, w_gate, w_up, w_down, g_final)


def _reorder_head_dims(a):
    lead = a.shape[:-1]
    a = a.reshape(*lead, -1, 2, 2, HEAD_DIM // 4)
    return jnp.swapaxes(a, -3, -2).reshape(*lead, -1)


def _reorder_qk_columns(a):
    n_qk = (N_HEADS + N_KV_HEADS) * HEAD_DIM
    return jnp.concatenate([_reorder_head_dims(a[..., :n_qk]), a[..., n_qk:]], axis=-1)


def _layer(x, mod, tabs, p):
    cos, sin = tabs
    qt, k, vt, u, kn = _in_proj(x, mod, p["g1"], p["w_in"], p["b_in"], p["gq"], p["gk"], cos, sin)
    attn = _attention(qt, k, vt, kn)
    x1, h2 = _out_proj(attn, u, x, mod, p["w_dw"], p["b_dw"], p["g_cn"], p["b_cn"], p["w_out"], p["g2"])
    return _ffn(h2, x1, mod, p["w_gate"], p["w_up"], p["w_down"], p["g_final"])


def kernel(x_prompt, x_sample, c_prompt, c_sample, w_ada, b_ada, g_norm1, w_in, b_in, g_q, g_k, w_dw, b_dw,
           g_cn, b_cn, w_out, g_norm2, w_gate, w_up, w_down, g_final):
    assert w_ada.shape[0] == 1, "single layer"
    assert x_prompt.shape[1] == x_sample.shape[1]
    seq = x_prompt.shape[1]
    assert seq % ROW_TILE == 0 and seq % Q_TILE == 0 and seq % GRID_W == 0
    nb_p, nb_s = x_prompt.shape[0], x_sample.shape[0]

    c_all = jnp.concatenate([c_prompt, c_sample], axis=0)
    pad = (-c_all.shape[0]) % 8
    c_pad = jnp.pad(c_all, ((0, pad), (0, 0)))
    mod = _modulation(c_pad, w_ada[0], b_ada[0][None, :]).reshape(c_pad.shape[0], 6, D_MODEL)

    p = {
        "g1": g_norm1[0][None, :],
        "w_in": _reorder_qk_columns(w_in[0].astype(BF16)),
        "b_in": _reorder_qk_columns(b_in[0])[None, :],
        "gq": _reorder_head_dims(g_q[0])[None, :],
        "gk": _reorder_head_dims(g_k[0])[None, :],
        "w_dw": w_dw[0],
        "b_dw": b_dw[0][None, :],
        "g_cn": g_cn[0][None, :],
        "b_cn": b_cn[0][None, :],
        "w_out": w_out[0].astype(BF16),
        "g2": g_norm2[0][None, :],
        "w_gate": w_gate[0].astype(BF16),
        "w_up": w_up[0].astype(BF16),
        "w_down": w_down[0].astype(BF16),
        "g_final": g_final[None, :],
    }
    tabs = _rope_tables(seq)
    y_prompt = _layer(x_prompt, mod[:nb_p], tabs, p)
    y_sample = _layer(x_sample, mod[nb_p:nb_p + nb_s], tabs, p)
    return (y_prompt, y_sample)
```

```python
import functools
import math

import jax
import jax.numpy as jnp
from jax import lax
from jax.experimental import pallas as pl
from jax.experimental.pallas import tpu as pltpu

F32 = jnp.float32
BF16 = jnp.bfloat16

D_MODEL = 2048
HEAD_DIM = 128
N_HEADS = 8
N_KV_HEADS = 2
GROUP = N_HEADS // N_KV_HEADS
ATTN_DIM = N_HEADS * HEAD_DIM
KV_DIM = N_KV_HEADS * HEAD_DIM
CONV_DIM = D_MODEL - ATTN_DIM
CONV_WIDTH = 31
CONV_PAD = CONV_WIDTH // 2
IN_DIM = ATTN_DIM + 2 * KV_DIM + 2 * CONV_DIM
D_FF = 5632
GRID_W = 64
ROPE_THETA = 10000.0
EPS = 1e-6

LANES = 128
SUBLANES = 8
BF16_ROWS = 16
MXU_DEPTH = 256
V7X_VMEM_LIMIT_BYTES = 56 * 1024 * 1024

ROW_TILE = 512
KV_CHUNK = 512
Q_TILE = 1024
FF_TILE = 512
HALO = 16
CONV_ROWS = 256
MOD_COLS = 1536

Q_SCALE = math.log2(math.e) / math.sqrt(HEAD_DIM)
SAFE_SHIFT = 48.0
NORM_MARGIN = 1.02


def _dot(a, b):
    return jnp.dot(a, b, preferred_element_type=F32)


def _params(*sem):
    return pltpu.CompilerParams(dimension_semantics=sem, vmem_limit_bytes=V7X_VMEM_LIMIT_BYTES)


def _const_spec(shape):
    nd = len(shape)
    return pl.BlockSpec(shape, lambda *_: (0,) * nd, pipeline_mode=pl.Buffered(1))


def _mod_kernel(c_ref, w_ref, b_ref, o_ref):
    c = c_ref[...]
    a = (c * jax.nn.sigmoid(c)).astype(BF16)
    o_ref[...] = _dot(a, w_ref[...].astype(BF16)) + b_ref[...]


def _modulation(c_pad, w_ada, b_ada):
    rows = c_pad.shape[0]
    n = w_ada.shape[1]
    return pl.pallas_call(
        _mod_kernel,
        grid=(n // MOD_COLS,),
        in_specs=[
            pl.BlockSpec((rows, D_MODEL), lambda j: (0, 0)),
            pl.BlockSpec((D_MODEL, MOD_COLS), lambda j: (0, j)),
            pl.BlockSpec((1, MOD_COLS), lambda j: (0, j)),
        ],
        out_specs=pl.BlockSpec((rows, MOD_COLS), lambda j: (0, j)),
        out_shape=jax.ShapeDtypeStruct((rows, n), F32),
        compiler_params=_params("arbitrary"),
        name="modulation",
    )(c_pad, w_ada, b_ada)


def _rope_kernel(cos_ref, sin_ref, *, n_rows):
    shape = cos_ref.shape
    r = lax.broadcasted_iota(jnp.int32, shape, 0)
    lane = lax.broadcasted_iota(jnp.int32, shape, 1)
    pos = jnp.where(r < n_rows, r, r - n_rows).astype(F32)
    freq = (lane % (HEAD_DIM // 4)).astype(F32)
    inv = jnp.exp(freq * (-2.0 / (HEAD_DIM // 2) * math.log(ROPE_THETA)))
    ang = pos * inv
    cos_ref[...] = jnp.cos(ang)
    sin_ref[...] = jnp.sin(ang)


def _rope_tables(seq):
    n_rows = seq // GRID_W
    n = n_rows + GRID_W
    cos, sin = pl.pallas_call(
        functools.partial(_rope_kernel, n_rows=n_rows),
        out_shape=(jax.ShapeDtypeStruct((n, LANES), F32), jax.ShapeDtypeStruct((n, LANES), F32)),
        name="rope_table",
    )()
    lane = jnp.arange(LANES)
    row_lane = ((lane // (HEAD_DIM // 4)) % 2 == 0)[None, :]
    sign = jnp.where(lane < HEAD_DIM // 2, -1.0, 1.0).astype(F32)[None, :]

    def expand(t):
        by_row = jnp.repeat(t[:n_rows], GRID_W, axis=0)
        by_col = jnp.tile(t[n_rows:], (n_rows, 1))
        return jnp.where(row_lane, by_row, by_col)

    return expand(cos), expand(sin) * sign


def _inproj_kernel(x_ref, mod_ref, g1_ref, w_ref, b_ref, gq_ref, gk_ref, cos_ref, sin_ref,
                   qt_ref, k_ref, vt_ref, u_ref, kn_ref):
    @pl.when(pl.program_id(1) == 0)
    def _():
        kn_ref[...] = jnp.zeros(kn_ref.shape, F32)

    x = x_ref[0]
    ms = jnp.mean(x * x, axis=-1, keepdims=True)
    sh1 = mod_ref[0, 0:1, :]
    sc1 = mod_ref[0, 1:2, :]
    h = (x * lax.rsqrt(ms + EPS) * g1_ref[...]) * (1.0 + sc1) + sh1
    hb = h.astype(BF16)
    cos = cos_ref[...]
    sin = sin_ref[...]

    def norm_rope(z, g):
        r = lax.rsqrt(jnp.mean(z * z, axis=-1, keepdims=True) + EPS)
        y = z * r * g
        return y * cos + pltpu.roll(y, HEAD_DIM // 2, 1) * sin

    k0 = ATTN_DIM
    v0 = ATTN_DIM + KV_DIM
    u0 = ATTN_DIM + 2 * KV_DIM
    zq = _dot(hb, w_ref[:, 0:k0]) + b_ref[:, 0:k0]
    zk = _dot(hb, w_ref[:, k0:v0]) + b_ref[:, k0:v0]
    zv = _dot(hb, w_ref[:, v0:u0]) + b_ref[:, v0:u0]
    za = _dot(hb, w_ref[:, u0:u0 + CONV_DIM]) + b_ref[:, u0:u0 + CONV_DIM]
    zg = _dot(hb, w_ref[:, u0 + CONV_DIM:]) + b_ref[:, u0 + CONV_DIM:]

    for hd in range(N_HEADS):
        sl = slice(hd * HEAD_DIM, (hd + 1) * HEAD_DIM)
        y = norm_rope(zq[:, sl], gq_ref[...]) * Q_SCALE
        qt_ref[0, sl, :] = y.T.astype(BF16)

    for hd in range(N_KV_HEADS):
        sl = slice(hd * HEAD_DIM, (hd + 1) * HEAD_DIM)
        kh = norm_rope(zk[:, sl], gk_ref[...])
        k_ref[0, :, sl] = kh.astype(BF16)
        n2 = jnp.max(jnp.sum(kh * kh, axis=-1, keepdims=True), axis=0, keepdims=True)
        kn_ref[0, hd:hd + 1, :] = jnp.maximum(kn_ref[0, hd:hd + 1, :], n2)

    vt_ref[0, 0] = zv.T.astype(BF16)

    glu = za * jax.nn.sigmoid(zg)
    for c in range(CONV_DIM // LANES):
        u_ref[0, c] = glu[:, c * LANES:(c + 1) * LANES]


def _in_proj(x, mod, g1, w_in, b_in, gq, gk, cos, sin):
    bsz, seq, _ = x.shape
    nt = seq // ROW_TILE
    return pl.pallas_call(
        _inproj_kernel,
        grid=(bsz, nt),
        in_specs=[
            pl.BlockSpec((1, ROW_TILE, D_MODEL), lambda b, i: (b, i, 0)),
            pl.BlockSpec((1, 6, D_MODEL), lambda b, i: (b, 0, 0)),
            _const_spec((1, D_MODEL)),
            _const_spec((D_MODEL, IN_DIM)),
            _const_spec((1, IN_DIM)),
            _const_spec((1, HEAD_DIM)),
            _const_spec((1, HEAD_DIM)),
            pl.BlockSpec((ROW_TILE, LANES), lambda b, i: (i, 0)),
            pl.BlockSpec((ROW_TILE, LANES), lambda b, i: (i, 0)),
        ],
        out_specs=[
            pl.BlockSpec((1, ATTN_DIM, ROW_TILE), lambda b, i: (b, 0, i)),
            pl.BlockSpec((1, ROW_TILE, KV_DIM), lambda b, i: (b, i, 0)),
            pl.BlockSpec((1, 1, KV_DIM, KV_CHUNK), lambda b, i: (b, i, 0, 0)),
            pl.BlockSpec((1, CONV_DIM // LANES, ROW_TILE, LANES), lambda b, i: (b, 0, i, 0)),
            pl.BlockSpec((1, SUBLANES, LANES), lambda b, i: (b, 0, 0)),
        ],
        out_shape=[
            jax.ShapeDtypeStruct((bsz, ATTN_DIM, seq), BF16),
            jax.ShapeDtypeStruct((bsz, seq, KV_DIM), BF16),
            jax.ShapeDtypeStruct((bsz, nt, KV_DIM, KV_CHUNK), BF16),
            jax.ShapeDtypeStruct((bsz, CONV_DIM // LANES, seq, LANES), F32),
            jax.ShapeDtypeStruct((bsz, SUBLANES, LANES), F32),
        ],
        compiler_params=_params("parallel", "arbitrary"),
        name="in_proj",
    )(x, mod, g1, w_in, b_in, gq, gk, cos, sin)


def _attn_kernel(qt_ref, k_ref, vt_ref, kn_ref, o_ref, qa_ref, p0_ref, p1_ref, m_ref, l_ref, acc_ref, *, n_chunks):
    h = pl.program_id(1)
    tq = qt_ref.shape[2]
    gw = GROUP * tq
    k_norm = jnp.sqrt(kn_ref[0, pl.ds(h, 1), :][:, 0:1])
    row = lax.broadcasted_iota(jnp.int32, (BF16_ROWS, tq), 0)
    worst = jnp.zeros((1, 1), F32)
    for g in range(GROUP):
        q = qt_ref[0, g * HEAD_DIM:(g + 1) * HEAD_DIM, :]
        qf = q.astype(F32)
        shift = jnp.sqrt(jnp.sum(qf * qf, axis=0, keepdims=True)) * k_norm * NORM_MARGIN
        worst = jnp.maximum(worst, jnp.max(shift, axis=1, keepdims=True))
        cs = slice(g * tq, (g + 1) * tq)
        qa_ref[0:HEAD_DIM, cs] = q
        qa_ref[HEAD_DIM:HEAD_DIM + BF16_ROWS, cs] = jnp.where(row == 0, -shift, 0.0).astype(BF16)
        qa_ref[HEAD_DIM + BF16_ROWS:, cs] = jnp.zeros((MXU_DEPTH - HEAD_DIM - BF16_ROWS, tq), BF16)
    safe = worst[0, 0] <= SAFE_SHIFT

    acc_ref[...] = jnp.zeros(acc_ref.shape, F32)

    @pl.when(safe)
    def _():
        lane = lax.broadcasted_iota(jnp.int32, (KV_CHUNK, LANES), 1)
        ones_col = jnp.where(lane == 0, 1.0, 0.0).astype(BF16)

        def qk_exp(c, p_ref):
            off = pl.multiple_of(c * KV_CHUNK, KV_CHUNK)
            ka = jnp.concatenate([k_ref[0, pl.ds(off, KV_CHUNK), :], ones_col], axis=1)
            p = jnp.exp2(_dot(ka, qa_ref[...]))
            p_ref[...] = p.astype(BF16)
            return jnp.sum(p.reshape(KV_CHUNK // SUBLANES, SUBLANES, gw), axis=0)

        def pv(c, p_ref):
            acc_ref[...] += _dot(vt_ref[0, c], p_ref[...])

        def body(i, l):
            c = 2 * i
            l = l + qk_exp(c + 1, p1_ref)
            pv(c, p0_ref)
            l = l + qk_exp(c + 2, p0_ref)
            pv(c + 1, p1_ref)
            return l

        l = lax.fori_loop(0, n_chunks // 2 - 1, body, qk_exp(0, p0_ref))
        l = l + qk_exp(n_chunks - 1, p1_ref)
        pv(n_chunks - 2, p0_ref)
        pv(n_chunks - 1, p1_ref)
        l_ref[...] = l

    @pl.when(jnp.logical_not(safe))
    def _():
        m_ref[...] = jnp.full(m_ref.shape, -jnp.inf, F32)
        l_ref[...] = jnp.zeros(l_ref.shape, F32)
        first_row = lax.broadcasted_iota(jnp.int32, (SUBLANES, tq), 0) == 0

        def body(c, carry):
            off = pl.multiple_of(c * KV_CHUNK, KV_CHUNK)
            kb = k_ref[0, pl.ds(off, KV_CHUNK), :]
            vb = vt_ref[0, c]
            for g in range(GROUP):
                cs = slice(g * tq, (g + 1) * tq)
                s = _dot(kb, qt_ref[0, g * HEAD_DIM:(g + 1) * HEAD_DIM, :])
                m_old = m_ref[:, cs]
                m_new = jnp.maximum(m_old, jnp.max(s, axis=0, keepdims=True))
                alpha = jnp.exp2(m_old - m_new)
                p = jnp.exp2(s - m_new)
                l_ref[:, cs] = alpha * l_ref[:, cs] + jnp.where(first_row, jnp.sum(p, axis=0, keepdims=True), 0.0)
                acc_ref[:, cs] = alpha * acc_ref[:, cs] + _dot(vb, p.astype(BF16))
                m_ref[:, cs] = m_new
            return carry

        lax.fori_loop(0, n_chunks, body, 0)

    o = acc_ref[...] * (1.0 / jnp.sum(l_ref[...], axis=0, keepdims=True))
    for g in range(GROUP):
        o_ref[0, :, g * HEAD_DIM:(g + 1) * HEAD_DIM] = o[:, g * tq:(g + 1) * tq].T.astype(BF16)


def _attention(qt, k, vt, kn):
    bsz, _, seq = qt.shape
    n_chunks = seq // KV_CHUNK
    assert n_chunks % 2 == 0 and n_chunks >= 4
    gw = GROUP * HEAD_DIM
    return pl.pallas_call(
        functools.partial(_attn_kernel, n_chunks=n_chunks),
        grid=(bsz, N_KV_HEADS, seq // Q_TILE),
        in_specs=[
            pl.BlockSpec((1, gw, Q_TILE), lambda b, h, i: (b, h, i)),
            pl.BlockSpec((1, seq, HEAD_DIM), lambda b, h, i: (b, 0, h)),
            pl.BlockSpec((1, n_chunks, HEAD_DIM, KV_CHUNK), lambda b, h, i: (b, 0, h, 0)),
            pl.BlockSpec((1, SUBLANES, LANES), lambda b, h, i: (b, 0, 0)),
        ],
        out_specs=pl.BlockSpec((1, Q_TILE, gw), lambda b, h, i: (b, i, h)),
        out_shape=jax.ShapeDtypeStruct((bsz, seq, ATTN_DIM), BF16),
        scratch_shapes=[
            pltpu.VMEM((MXU_DEPTH, GROUP * Q_TILE), BF16),
            pltpu.VMEM((KV_CHUNK, GROUP * Q_TILE), BF16),
            pltpu.VMEM((KV_CHUNK, GROUP * Q_TILE), BF16),
            pltpu.VMEM((1, GROUP * Q_TILE), F32),
            pltpu.VMEM((SUBLANES, GROUP * Q_TILE), F32),
            pltpu.VMEM((HEAD_DIM, GROUP * Q_TILE), F32),
        ],
        compiler_params=_params("parallel", "parallel", "arbitrary"),
        name="attention",
    )(qt, k, vt, kn)


def _outproj_kernel(attn_ref, u_ref, up_ref, un_ref, x_ref, mod_ref, wdw_ref, bdw_ref, gcn_ref, bcn_ref,
                    wout_ref, g2_ref, x1_ref, h2_ref, buf_ref):
    i = pl.program_id(1)
    last = pl.num_programs(1) - 1
    buf_ref[:, 0:HALO, :] = jnp.where(i > 0, up_ref[0], 0.0)
    buf_ref[:, HALO:HALO + ROW_TILE, :] = u_ref[0]
    buf_ref[:, HALO + ROW_TILE:, :] = jnp.where(i < last, un_ref[0], 0.0)

    gt1 = mod_ref[0, 2:3, :]
    sh2 = mod_ref[0, 3:4, :]
    sc2 = mod_ref[0, 4:5, :]
    base = HALO - CONV_PAD
    for r0 in range(0, ROW_TILE, CONV_ROWS):
        rows = slice(r0, r0 + CONV_ROWS)
        cols = []
        for c in range(CONV_DIM // LANES):
            cs = slice(c * LANES, (c + 1) * LANES)
            acc = jnp.broadcast_to(bdw_ref[:, cs], (CONV_ROWS, LANES))
            for j in range(CONV_WIDTH):
                lo = base + r0 + j
                acc = acc + buf_ref[c, lo:lo + CONV_ROWS, :] * wdw_ref[j:j + 1, cs]
            cols.append(acc)
        hc = jnp.concatenate(cols, axis=1)
        mu = jnp.mean(hc, axis=-1, keepdims=True)
        d = hc - mu
        var = jnp.mean(d * d, axis=-1, keepdims=True)
        y = d * lax.rsqrt(var + EPS) * gcn_ref[...] + bcn_ref[...]
        conv = (y * jax.nn.sigmoid(y)).astype(BF16)

        mix = _dot(attn_ref[0, rows, :], wout_ref[0:ATTN_DIM, :]) + _dot(conv, wout_ref[ATTN_DIM:, :])
        x1 = x_ref[0, rows, :] + gt1 * mix
        x1_ref[0, rows, :] = x1
        ms = jnp.mean(x1 * x1, axis=-1, keepdims=True)
        h2 = (x1 * lax.rsqrt(ms + EPS) * g2_ref[...]) * (1.0 + sc2) + sh2
        h2_ref[0, rows, :] = h2.astype(BF16)


def _out_proj(attn, u, x, mod, w_dw, b_dw, g_cn, b_cn, w_out, g2):
    bsz, seq, _ = x.shape
    nt = seq // ROW_TILE
    hb = ROW_TILE // HALO
    n_halo = seq // HALO
    n_slab = CONV_DIM // LANES
    return pl.pallas_call(
        _outproj_kernel,
        grid=(bsz, nt),
        in_specs=[
            pl.BlockSpec((1, ROW_TILE, ATTN_DIM), lambda b, i: (b, i, 0)),
            pl.BlockSpec((1, n_slab, ROW_TILE, LANES), lambda b, i: (b, 0, i, 0)),
            pl.BlockSpec((1, n_slab, HALO, LANES), lambda b, i: (b, 0, jnp.maximum(i * hb - 1, 0), 0)),
            pl.BlockSpec((1, n_slab, HALO, LANES), lambda b, i: (b, 0, jnp.minimum((i + 1) * hb, n_halo - 1), 0)),
            pl.BlockSpec((1, ROW_TILE, D_MODEL), lambda b, i: (b, i, 0)),
            pl.BlockSpec((1, 6, D_MODEL), lambda b, i: (b, 0, 0)),
            _const_spec((CONV_WIDTH, CONV_DIM)),
            _const_spec((1, CONV_DIM)),
            _const_spec((1, CONV_DIM)),
            _const_spec((1, CONV_DIM)),
            _const_spec((D_MODEL, D_MODEL)),
            _const_spec((1, D_MODEL)),
        ],
        out_specs=[
            pl.BlockSpec((1, ROW_TILE, D_MODEL), lambda b, i: (b, i, 0)),
            pl.BlockSpec((1, ROW_TILE, D_MODEL), lambda b, i: (b, i, 0)),
        ],
        out_shape=[
            jax.ShapeDtypeStruct((bsz, seq, D_MODEL), F32),
            jax.ShapeDtypeStruct((bsz, seq, D_MODEL), BF16),
        ],
        scratch_shapes=[
            pltpu.VMEM((n_slab, ROW_TILE + 2 * HALO, LANES), F32),
        ],
        compiler_params=_params("parallel", "parallel"),
        name="out_proj",
    )(attn, u, u, u, x, mod, w_dw, b_dw, g_cn, b_cn, w_out, g2)


def _ffn_kernel(h2_ref, x1_ref, mod_ref, wg_ref, wu_ref, wd_ref, gf_ref, y_ref, acc_ref):
    j = pl.program_id(2)

    @pl.when(j == 0)
    def _():
        acc_ref[...] = jnp.zeros(acc_ref.shape, F32)

    h = h2_ref[0]
    g = _dot(h, wg_ref[...])
    u = _dot(h, wu_ref[...])
    a = (g * jax.nn.sigmoid(g) * u).astype(BF16)
    acc_ref[...] += _dot(a, wd_ref[...])

    @pl.when(j == pl.num_programs(2) - 1)
    def _():
        gt2 = mod_ref[0, 5:6, :]
        x2 = x1_ref[0] + gt2 * acc_ref[...]
        ms = jnp.mean(x2 * x2, axis=-1, keepdims=True)
        y_ref[0] = x2 * lax.rsqrt(ms + EPS) * gf_ref[...]


def _ffn(h2, x1, mod, w_gate, w_up, w_down, g_final):
    bsz, seq, _ = x1.shape
    return pl.pallas_call(
        _ffn_kernel,
        grid=(bsz, seq // ROW_TILE, D_FF // FF_TILE),
        in_specs=[
            pl.BlockSpec((1, ROW_TILE, D_MODEL), lambda b, i, j: (b, i, 0)),
            pl.BlockSpec((1, ROW_TILE, D_MODEL), lambda b, i, j: (b, i, 0)),
            pl.BlockSpec((1, 6, D_MODEL), lambda b, i, j: (b, 0, 0)),
            pl.BlockSpec((D_MODEL, FF_TILE), lambda b, i, j: (0, j)),
            pl.BlockSpec((D_MODEL, FF_TILE), lambda b, i, j: (0, j)),
            pl.BlockSpec((FF_TILE, D_MODEL), lambda b, i, j: (j, 0)),
            _const_spec((1, D_MODEL)),
        ],
        out_specs=pl.BlockSpec((1, ROW_TILE, D_MODEL), lambda b, i, j: (b, i, 0)),
        out_shape=jax.ShapeDtypeStruct((bsz, seq, D_MODEL), F32),
        scratch_shapes=[pltpu.VMEM((ROW_TILE, D_MODEL), F32)],
        compiler_params=_params("parallel", "parallel", "arbitrary"),
        name="ffn",
    )(h2, x1, mod, w_gate, w_up, w_down, g_final)


def _reorder_head_dims(a):
    lead = a.shape[:-1]
    a = a.reshape(*lead, -1, 2, 2, HEAD_DIM // 4)
    return jnp.swapaxes(a, -3, -2).reshape(*lead, -1)


def _reorder_qk_columns(a):
    n_qk = (N_HEADS + N_KV_HEADS) * HEAD_DIM
    return jnp.concatenate([_reorder_head_dims(a[..., :n_qk]), a[..., n_qk:]], axis=-1)


def _layer(x, mod, tabs, p):
    cos, sin = tabs
    qt, k, vt, u, kn = _in_proj(x, mod, p["g1"], p["w_in"], p["b_in"], p["gq"], p["gk"], cos, sin)
    attn = _attention(qt, k, vt, kn)
    x1, h2 = _out_proj(attn, u, x, mod, p["w_dw"], p["b_dw"], p["g_cn"], p["b_cn"], p["w_out"], p["g2"])
    return _ffn(h2, x1, mod, p["w_gate"], p["w_up"], p["w_down"], p["g_final"])


def kernel(x_prompt, x_sample, c_prompt, c_sample, w_ada, b_ada, g_norm1, w_in, b_in, g_q, g_k, w_dw, b_dw,
           g_cn, b_cn, w_out, g_norm2, w_gate, w_up, w_down, g_final):
    assert w_ada.shape[0] == 1, "single layer"
    assert x_prompt.shape[1] == x_sample.shape[1]
    seq = x_prompt.shape[1]
    assert seq % ROW_TILE == 0 and seq % Q_TILE == 0 and seq % GRID_W == 0
    nb_p, nb_s = x_prompt.shape[0], x_sample.shape[0]

    c_all = jnp.concatenate([c_prompt, c_sample], axis=0)
    pad = (-c_all.shape[0]) % 8
    c_pad = jnp.pad(c_all, ((0, pad), (0, 0)))
    mod = _modulation(c_pad, w_ada[0], b_ada[0][None, :]).reshape(c_pad.shape[0], 6, D_MODEL)

    p = {
        "g1": g_norm1[0][None, :],
        "w_in": _reorder_qk_columns(w_in[0].astype(BF16)),
        "b_in": _reorder_qk_columns(b_in[0])[None, :],
        "gq": _reorder_head_dims(g_q[0])[None, :],
        "gk": _reorder_head_dims(g_k[0])[None, :],
        "w_dw": w_dw[0],
        "b_dw": b_dw[0][None, :],
        "g_cn": g_cn[0][None, :],
        "b_cn": b_cn[0][None, :],
        "w_out": w_out[0].astype(BF16),
        "g2": g_norm2[0][None, :],
        "w_gate": w_gate[0].astype(BF16),
        "w_up": w_up[0].astype(BF16),
        "w_down": w_down[0].astype(BF16),
        "g_final": g_final[None, :],
    }
    tabs = _rope_tables(seq)
    y_prompt = _layer(x_prompt, mod[:nb_p], tabs, p)
    y_sample = _layer(x_sample, mod[nb_p:nb_p + nb_s], tabs, p)
    return (y_prompt, y_sample)
```

```python
import functools
import math

import jax
import jax.numpy as jnp
from jax import lax
from jax.experimental import pallas as pl
from jax.experimental.pallas import tpu as pltpu

F32 = jnp.float32
BF16 = jnp.bfloat16

D_MODEL = 2048
HEAD_DIM = 128
N_HEADS = 8
N_KV_HEADS = 2
GROUP = N_HEADS // N_KV_HEADS
ATTN_DIM = N_HEADS * HEAD_DIM
KV_DIM = N_KV_HEADS * HEAD_DIM
CONV_DIM = D_MODEL - ATTN_DIM
CONV_WIDTH = 31
CONV_PAD = CONV_WIDTH // 2
IN_DIM = ATTN_DIM + 2 * KV_DIM + 2 * CONV_DIM
D_FF = 5632
GRID_W = 64
ROPE_THETA = 10000.0
EPS = 1e-6

LANES = 128
SUBLANES = 8
BF16_ROWS = 16
MXU_DEPTH = 256
V7X_VMEM_LIMIT_BYTES = 56 * 1024 * 1024
V7X_FFN_VMEM_LIMIT_BYTES = 60 * 1024 * 1024

ROW_TILE = 512
KV_CHUNK = 512
Q_TILE = 1024
FF_TILE = 512
FF_ROW_TILE = 1024
X1_PIECES = 8
X1_PIECE_ROWS = FF_ROW_TILE // X1_PIECES
HALO = 16
CONV_ROWS = 256
MOD_COLS = 1536

Q_SCALE = math.log2(math.e) / math.sqrt(HEAD_DIM)
SAFE_SHIFT = 48.0
NORM_MARGIN = 1.02


def _dot(a, b):
    return jnp.dot(a, b, preferred_element_type=F32)


def _params(*sem):
    return pltpu.CompilerParams(dimension_semantics=sem, vmem_limit_bytes=V7X_VMEM_LIMIT_BYTES)


def _const_spec(shape):
    nd = len(shape)
    return pl.BlockSpec(shape, lambda *_: (0,) * nd, pipeline_mode=pl.Buffered(1))


def _mod_kernel(c_ref, w_ref, b_ref, o_ref):
    c = c_ref[...]
    a = (c * jax.nn.sigmoid(c)).astype(BF16)
    o_ref[...] = _dot(a, w_ref[...].astype(BF16)) + b_ref[...]


def _modulation(c_pad, w_ada, b_ada):
    rows = c_pad.shape[0]
    n = w_ada.shape[1]
    return pl.pallas_call(
        _mod_kernel,
        grid=(n // MOD_COLS,),
        in_specs=[
            pl.BlockSpec((rows, D_MODEL), lambda j: (0, 0)),
            pl.BlockSpec((D_MODEL, MOD_COLS), lambda j: (0, j)),
            pl.BlockSpec((1, MOD_COLS), lambda j: (0, j)),
        ],
        out_specs=pl.BlockSpec((rows, MOD_COLS), lambda j: (0, j)),
        out_shape=jax.ShapeDtypeStruct((rows, n), F32),
        compiler_params=_params("arbitrary"),
        name="modulation",
    )(c_pad, w_ada, b_ada)


def _rope_kernel(cos_ref, sin_ref, tc_ref, ts_ref, *, n_rows):
    shape = tc_ref.shape
    r = lax.broadcasted_iota(jnp.int32, shape, 0)
    lane = lax.broadcasted_iota(jnp.int32, shape, 1)
    pos = jnp.where(r < n_rows, r, r - n_rows).astype(F32)
    freq = (lane % (HEAD_DIM // 4)).astype(F32)
    inv = jnp.exp(freq * (-2.0 / (HEAD_DIM // 2) * math.log(ROPE_THETA)))
    ang = pos * inv
    tc_ref[...] = jnp.cos(ang)
    ts_ref[...] = jnp.where(lane < HEAD_DIM // 2, -1.0, 1.0) * jnp.sin(ang)

    lane_b = lax.broadcasted_iota(jnp.int32, (GRID_W, LANES), 1)
    row_lane = (lane_b // (HEAD_DIM // 4)) % 2 == 0
    col_c = tc_ref[n_rows:, :]
    col_s = ts_ref[n_rows:, :]

    def body(g, carry):
        rows = pl.ds(pl.multiple_of(g * GRID_W, GRID_W), GRID_W)
        cos_ref[rows, :] = jnp.where(row_lane, tc_ref[pl.ds(g, 1), :], col_c)
        sin_ref[rows, :] = jnp.where(row_lane, ts_ref[pl.ds(g, 1), :], col_s)
        return carry

    lax.fori_loop(0, n_rows, body, 0)


def _rope_tables(seq):
    n_rows = seq // GRID_W
    table = jax.ShapeDtypeStruct((seq, LANES), F32)
    small = pltpu.VMEM((n_rows + GRID_W, LANES), F32)
    return pl.pallas_call(
        functools.partial(_rope_kernel, n_rows=n_rows),
        out_shape=(table, table),
        scratch_shapes=[small, small],
        compiler_params=pltpu.CompilerParams(vmem_limit_bytes=V7X_VMEM_LIMIT_BYTES),
        name="rope_table",
    )()


def _inproj_kernel(x_ref, mod_ref, g1_ref, w_ref, b_ref, gq_ref, gk_ref, cos_ref, sin_ref,
                   qt_ref, k_ref, vt_ref, u_ref, kn_ref):
    @pl.when(pl.program_id(1) == 0)
    def _():
        kn_ref[...] = jnp.zeros(kn_ref.shape, F32)

    x = x_ref[0]
    ms = jnp.mean(x * x, axis=-1, keepdims=True)
    sh1 = mod_ref[0, 0:1, :]
    sc1 = mod_ref[0, 1:2, :]
    h = (x * lax.rsqrt(ms + EPS) * g1_ref[...]) * (1.0 + sc1) + sh1
    hb = h.astype(BF16)
    cos = cos_ref[...]
    sin = sin_ref[...]

    def norm_rope(z, g):
        r = lax.rsqrt(jnp.mean(z * z, axis=-1, keepdims=True) + EPS)
        y = z * r * g
        return y * cos + pltpu.roll(y, HEAD_DIM // 2, 1) * sin

    k0 = ATTN_DIM
    v0 = ATTN_DIM + KV_DIM
    u0 = ATTN_DIM + 2 * KV_DIM
    zq = _dot(hb, w_ref[:, 0:k0]) + b_ref[:, 0:k0]
    zk = _dot(hb, w_ref[:, k0:v0]) + b_ref[:, k0:v0]
    zv = _dot(hb, w_ref[:, v0:u0]) + b_ref[:, v0:u0]
    za = _dot(hb, w_ref[:, u0:u0 + CONV_DIM]) + b_ref[:, u0:u0 + CONV_DIM]
    zg = _dot(hb, w_ref[:, u0 + CONV_DIM:]) + b_ref[:, u0 + CONV_DIM:]

    for hd in range(N_HEADS):
        sl = slice(hd * HEAD_DIM, (hd + 1) * HEAD_DIM)
        y = norm_rope(zq[:, sl], gq_ref[...]) * Q_SCALE
        qt_ref[0, sl, :] = y.T.astype(BF16)

    for hd in range(N_KV_HEADS):
        sl = slice(hd * HEAD_DIM, (hd + 1) * HEAD_DIM)
        kh = norm_rope(zk[:, sl], gk_ref[...])
        k_ref[0, :, sl] = kh.astype(BF16)
        n2 = jnp.max(jnp.sum(kh * kh, axis=-1, keepdims=True), axis=0, keepdims=True)
        kn_ref[0, hd:hd + 1, :] = jnp.maximum(kn_ref[0, hd:hd + 1, :], n2)

    vt_ref[0, 0] = zv.T.astype(BF16)

    glu = za * jax.nn.sigmoid(zg)
    for c in range(CONV_DIM // LANES):
        u_ref[0, c] = glu[:, c * LANES:(c + 1) * LANES]


def _in_proj(x, mod, g1, w_in, b_in, gq, gk, cos, sin):
    bsz, seq, _ = x.shape
    nt = seq // ROW_TILE
    return pl.pallas_call(
        _inproj_kernel,
        grid=(bsz, nt),
        in_specs=[
            pl.BlockSpec((1, ROW_TILE, D_MODEL), lambda b, i: (b, i, 0)),
            pl.BlockSpec((1, 6, D_MODEL), lambda b, i: (b, 0, 0)),
            _const_spec((1, D_MODEL)),
            _const_spec((D_MODEL, IN_DIM)),
            _const_spec((1, IN_DIM)),
            _const_spec((1, HEAD_DIM)),
            _const_spec((1, HEAD_DIM)),
            pl.BlockSpec((ROW_TILE, LANES), lambda b, i: (i, 0)),
            pl.BlockSpec((ROW_TILE, LANES), lambda b, i: (i, 0)),
        ],
        out_specs=[
            pl.BlockSpec((1, ATTN_DIM, ROW_TILE), lambda b, i: (b, 0, i)),
            pl.BlockSpec((1, ROW_TILE, KV_DIM), lambda b, i: (b, i, 0)),
            pl.BlockSpec((1, 1, KV_DIM, KV_CHUNK), lambda b, i: (b, i, 0, 0)),
            pl.BlockSpec((1, CONV_DIM // LANES, ROW_TILE, LANES), lambda b, i: (b, 0, i, 0)),
            pl.BlockSpec((1, SUBLANES, LANES), lambda b, i: (b, 0, 0)),
        ],
        out_shape=[
            jax.ShapeDtypeStruct((bsz, ATTN_DIM, seq), BF16),
            jax.ShapeDtypeStruct((bsz, seq, KV_DIM), BF16),
            jax.ShapeDtypeStruct((bsz, nt, KV_DIM, KV_CHUNK), BF16),
            jax.ShapeDtypeStruct((bsz, CONV_DIM // LANES, seq, LANES), F32),
            jax.ShapeDtypeStruct((bsz, SUBLANES, LANES), F32),
        ],
        compiler_params=_params("parallel", "arbitrary"),
        name="in_proj",
    )(x, mod, g1, w_in, b_in, gq, gk, cos, sin)


def _attn_kernel(qt_ref, k_ref, vt_ref, kn_ref, o_ref, qa_ref, p0_ref, p1_ref, m_ref, l_ref, acc_ref, *, n_chunks):
    h = pl.program_id(1)
    tq = qt_ref.shape[2]
    gw = GROUP * tq
    k_norm = jnp.sqrt(kn_ref[0, pl.ds(h, 1), :][:, 0:1])
    row = lax.broadcasted_iota(jnp.int32, (BF16_ROWS, tq), 0)
    worst = jnp.zeros((1, 1), F32)
    for g in range(GROUP):
        q = qt_ref[0, g * HEAD_DIM:(g + 1) * HEAD_DIM, :]
        qf = q.astype(F32)
        shift = jnp.sqrt(jnp.sum(qf * qf, axis=0, keepdims=True)) * k_norm * NORM_MARGIN
        worst = jnp.maximum(worst, jnp.max(shift, axis=1, keepdims=True))
        cs = slice(g * tq, (g + 1) * tq)
        qa_ref[0:HEAD_DIM, cs] = q
        qa_ref[HEAD_DIM:HEAD_DIM + BF16_ROWS, cs] = jnp.where(row == 0, -shift, 0.0).astype(BF16)
        qa_ref[HEAD_DIM + BF16_ROWS:, cs] = jnp.zeros((MXU_DEPTH - HEAD_DIM - BF16_ROWS, tq), BF16)
    safe = worst[0, 0] <= SAFE_SHIFT

    acc_ref[...] = jnp.zeros(acc_ref.shape, F32)

    @pl.when(safe)
    def _():
        lane = lax.broadcasted_iota(jnp.int32, (KV_CHUNK, LANES), 1)
        ones_col = jnp.where(lane == 0, 1.0, 0.0).astype(BF16)

        def qk_exp(c, p_ref):
            off = pl.multiple_of(c * KV_CHUNK, KV_CHUNK)
            ka = jnp.concatenate([k_ref[0, pl.ds(off, KV_CHUNK), :], ones_col], axis=1)
            p = jnp.exp2(_dot(ka, qa_ref[...]))
            p_ref[...] = p.astype(BF16)
            return jnp.sum(p.reshape(KV_CHUNK // SUBLANES, SUBLANES, gw), axis=0)

        def pv(c, p_ref):
            acc_ref[...] += _dot(vt_ref[0, c], p_ref[...])

        def body(i, l):
            c = 2 * i
            l = l + qk_exp(c + 1, p1_ref)
            pv(c, p0_ref)
            l = l + qk_exp(c + 2, p0_ref)
            pv(c + 1, p1_ref)
            return l

        l = lax.fori_loop(0, n_chunks // 2 - 1, body, qk_exp(0, p0_ref))
        l = l + qk_exp(n_chunks - 1, p1_ref)
        pv(n_chunks - 2, p0_ref)
        pv(n_chunks - 1, p1_ref)
        l_ref[...] = l

    @pl.when(jnp.logical_not(safe))
    def _():
        m_ref[...] = jnp.full(m_ref.shape, -jnp.inf, F32)
        l_ref[...] = jnp.zeros(l_ref.shape, F32)
        first_row = lax.broadcasted_iota(jnp.int32, (SUBLANES, tq), 0) == 0

        def body(c, carry):
            off = pl.multiple_of(c * KV_CHUNK, KV_CHUNK)
            kb = k_ref[0, pl.ds(off, KV_CHUNK), :]
            vb = vt_ref[0, c]
            for g in range(GROUP):
                cs = slice(g * tq, (g + 1) * tq)
                s = _dot(kb, qt_ref[0, g * HEAD_DIM:(g + 1) * HEAD_DIM, :])
                m_old = m_ref[:, cs]
                m_new = jnp.maximum(m_old, jnp.max(s, axis=0, keepdims=True))
                alpha = jnp.exp2(m_old - m_new)
                p = jnp.exp2(s - m_new)
                l_ref[:, cs] = alpha * l_ref[:, cs] + jnp.where(first_row, jnp.sum(p, axis=0, keepdims=True), 0.0)
                acc_ref[:, cs] = alpha * acc_ref[:, cs] + _dot(vb, p.astype(BF16))
                m_ref[:, cs] = m_new
            return carry

        lax.fori_loop(0, n_chunks, body, 0)

    o = acc_ref[...] * (1.0 / jnp.sum(l_ref[...], axis=0, keepdims=True))
    for g in range(GROUP):
        o_ref[0, :, g * HEAD_DIM:(g + 1) * HEAD_DIM] = o[:, g * tq:(g + 1) * tq].T.astype(BF16)


def _attention(qt, k, vt, kn):
    bsz, _, seq = qt.shape
    n_chunks = seq // KV_CHUNK
    assert n_chunks % 2 == 0 and n_chunks >= 4
    gw = GROUP * HEAD_DIM
    return pl.pallas_call(
        functools.partial(_attn_kernel, n_chunks=n_chunks),
        grid=(bsz, N_KV_HEADS, seq // Q_TILE),
        in_specs=[
            pl.BlockSpec((1, gw, Q_TILE), lambda b, h, i: (b, h, i)),
            pl.BlockSpec((1, seq, HEAD_DIM), lambda b, h, i: (b, 0, h)),
            pl.BlockSpec((1, n_chunks, HEAD_DIM, KV_CHUNK), lambda b, h, i: (b, 0, h, 0)),
            pl.BlockSpec((1, SUBLANES, LANES), lambda b, h, i: (b, 0, 0)),
        ],
        out_specs=pl.BlockSpec((1, Q_TILE, gw), lambda b, h, i: (b, i, h)),
        out_shape=jax.ShapeDtypeStruct((bsz, seq, ATTN_DIM), BF16),
        scratch_shapes=[
            pltpu.VMEM((MXU_DEPTH, GROUP * Q_TILE), BF16),
            pltpu.VMEM((KV_CHUNK, GROUP * Q_TILE), BF16),
            pltpu.VMEM((KV_CHUNK, GROUP * Q_TILE), BF16),
            pltpu.VMEM((1, GROUP * Q_TILE), F32),
            pltpu.VMEM((SUBLANES, GROUP * Q_TILE), F32),
            pltpu.VMEM((HEAD_DIM, GROUP * Q_TILE), F32),
        ],
        compiler_params=_params("parallel", "parallel", "arbitrary"),
        name="attention",
    )(qt, k, vt, kn)


def _outproj_kernel(attn_ref, u_ref, up_ref, un_ref, x_ref, mod_ref, wdw_ref, bdw_ref, gcn_ref, bcn_ref,
                    wout_ref, g2_ref, x1_ref, h2_ref, buf_ref):
    i = pl.program_id(1)
    last = pl.num_programs(1) - 1
    buf_ref[:, 0:HALO, :] = jnp.where(i > 0, up_ref[0], 0.0)
    buf_ref[:, HALO:HALO + ROW_TILE, :] = u_ref[0]
    buf_ref[:, HALO + ROW_TILE:, :] = jnp.where(i < last, un_ref[0], 0.0)

    gt1 = mod_ref[0, 2:3, :]
    sh2 = mod_ref[0, 3:4, :]
    sc2 = mod_ref[0, 4:5, :]
    base = HALO - CONV_PAD
    for r0 in range(0, ROW_TILE, CONV_ROWS):
        rows = slice(r0, r0 + CONV_ROWS)
        cols = []
        for c in range(CONV_DIM // LANES):
            cs = slice(c * LANES, (c + 1) * LANES)
            acc = jnp.broadcast_to(bdw_ref[:, cs], (CONV_ROWS, LANES))
            for j in range(CONV_WIDTH):
                lo = base + r0 + j
                acc = acc + buf_ref[c, lo:lo + CONV_ROWS, :] * wdw_ref[j:j + 1, cs]
            cols.append(acc)
        hc = jnp.concatenate(cols, axis=1)
        mu = jnp.mean(hc, axis=-1, keepdims=True)
        d = hc - mu
        var = jnp.mean(d * d, axis=-1, keepdims=True)
        y = d * lax.rsqrt(var + EPS) * gcn_ref[...] + bcn_ref[...]
        conv = (y * jax.nn.sigmoid(y)).astype(BF16)

        mix = _dot(attn_ref[0, rows, :], wout_ref[0:ATTN_DIM, :]) + _dot(conv, wout_ref[ATTN_DIM:, :])
        x1 = x_ref[0, rows, :] + gt1 * mix
        x1_ref[0, rows, :] = x1
        ms = jnp.mean(x1 * x1, axis=-1, keepdims=True)
        h2 = (x1 * lax.rsqrt(ms + EPS) * g2_ref[...]) * (1.0 + sc2) + sh2
        h2_ref[0, rows, :] = h2.astype(BF16)


def _out_proj(attn, u, x, mod, w_dw, b_dw, g_cn, b_cn, w_out, g2):
    bsz, seq, _ = x.shape
    nt = seq // ROW_TILE
    hb = ROW_TILE // HALO
    n_halo = seq // HALO
    n_slab = CONV_DIM // LANES
    return pl.pallas_call(
        _outproj_kernel,
        grid=(bsz, nt),
        in_specs=[
            pl.BlockSpec((1, ROW_TILE, ATTN_DIM), lambda b, i: (b, i, 0)),
            pl.BlockSpec((1, n_slab, ROW_TILE, LANES), lambda b, i: (b, 0, i, 0)),
            pl.BlockSpec((1, n_slab, HALO, LANES), lambda b, i: (b, 0, jnp.maximum(i * hb - 1, 0), 0)),
            pl.BlockSpec((1, n_slab, HALO, LANES), lambda b, i: (b, 0, jnp.minimum((i + 1) * hb, n_halo - 1), 0)),
            pl.BlockSpec((1, ROW_TILE, D_MODEL), lambda b, i: (b, i, 0)),
            pl.BlockSpec((1, 6, D_MODEL), lambda b, i: (b, 0, 0)),
            _const_spec((CONV_WIDTH, CONV_DIM)),
            _const_spec((1, CONV_DIM)),
            _const_spec((1, CONV_DIM)),
            _const_spec((1, CONV_DIM)),
            _const_spec((D_MODEL, D_MODEL)),
            _const_spec((1, D_MODEL)),
        ],
        out_specs=[
            pl.BlockSpec((1, ROW_TILE, D_MODEL), lambda b, i: (b, i, 0)),
            pl.BlockSpec((1, ROW_TILE, D_MODEL), lambda b, i: (b, i, 0)),
        ],
        out_shape=[
            jax.ShapeDtypeStruct((bsz, seq, D_MODEL), F32),
            jax.ShapeDtypeStruct((bsz, seq, D_MODEL), BF16),
        ],
        scratch_shapes=[
            pltpu.VMEM((n_slab, ROW_TILE + 2 * HALO, LANES), F32),
        ],
        compiler_params=_params("parallel", "parallel"),
        name="out_proj",
    )(attn, u, u, u, x, mod, w_dw, b_dw, g_cn, b_cn, w_out, g2)


def _ffn_kernel(h2_ref, x1p_ref, mod_ref, wg_ref, wu_ref, wd_ref, gf_ref, y_ref, x1_ref):
    j = pl.program_id(2)

    @pl.when(j == 0)
    def _():
        y_ref[...] = jnp.zeros(y_ref.shape, F32)

    @pl.when(j < X1_PIECES)
    def _():
        x1_ref[pl.ds(pl.multiple_of(j * X1_PIECE_ROWS, X1_PIECE_ROWS), X1_PIECE_ROWS), :] = x1p_ref[0]

    h = h2_ref[0]
    g = _dot(h, wg_ref[...])
    u = _dot(h, wu_ref[...])
    a = (g * jax.nn.sigmoid(g) * u).astype(BF16)
    y_ref[0] += _dot(a, wd_ref[...])

    @pl.when(j == pl.num_programs(2) - 1)
    def _():
        gt2 = mod_ref[0, 5:6, :]

        def finish(r, carry):
            rows = pl.ds(pl.multiple_of(r * X1_PIECE_ROWS, X1_PIECE_ROWS), X1_PIECE_ROWS)
            x2 = x1_ref[rows, :] + gt2 * y_ref[0, rows, :]
            ms = jnp.mean(x2 * x2, axis=-1, keepdims=True)
            y_ref[0, rows, :] = x2 * lax.rsqrt(ms + EPS) * gf_ref[...]
            return carry

        lax.fori_loop(0, X1_PIECES, finish, 0)


def _ffn(h2, x1, mod, w_gate, w_up, w_down, g_final):
    bsz, seq, _ = x1.shape
    n_ff = D_FF // FF_TILE
    assert n_ff >= X1_PIECES
    return pl.pallas_call(
        _ffn_kernel,
        grid=(bsz, seq // FF_ROW_TILE, n_ff),
        in_specs=[
            pl.BlockSpec((1, FF_ROW_TILE, D_MODEL), lambda b, i, j: (b, i, 0)),
            pl.BlockSpec((1, X1_PIECE_ROWS, D_MODEL),
                         lambda b, i, j: (b, i * X1_PIECES + jnp.minimum(j, X1_PIECES - 1), 0)),
            pl.BlockSpec((1, 6, D_MODEL), lambda b, i, j: (b, 0, 0)),
            pl.BlockSpec((D_MODEL, FF_TILE), lambda b, i, j: (0, j)),
            pl.BlockSpec((D_MODEL, FF_TILE), lambda b, i, j: (0, j)),
            pl.BlockSpec((FF_TILE, D_MODEL), lambda b, i, j: (j, 0)),
            _const_spec((1, D_MODEL)),
        ],
        out_specs=pl.BlockSpec((1, FF_ROW_TILE, D_MODEL), lambda b, i, j: (b, i, 0)),
        out_shape=jax.ShapeDtypeStruct((bsz, seq, D_MODEL), F32),
        scratch_shapes=[pltpu.VMEM((FF_ROW_TILE, D_MODEL), F32)],
        compiler_params=pltpu.CompilerParams(dimension_semantics=("parallel", "parallel", "arbitrary"),
                                             vmem_limit_bytes=V7X_FFN_VMEM_LIMIT_BYTES),
        name="ffn",
    )(h2, x1, mod, w_gate, w_up, w_down, g_final)


def _reorder_head_dims(a):
    lead = a.shape[:-1]
    a = a.reshape(*lead, -1, 2, 2, HEAD_DIM // 4)
    return jnp.swapaxes(a, -3, -2).reshape(*lead, -1)


def _reorder_qk_columns(a):
    n_qk = (N_HEADS + N_KV_HEADS) * HEAD_DIM
    return jnp.concatenate([_reorder_head_dims(a[..., :n_qk]), a[..., n_qk:]], axis=-1)


def _layer(x, mod, tabs, p):
    cos, sin = tabs
    qt, k, vt, u, kn = _in_proj(x, mod, p["g1"], p["w_in"], p["b_in"], p["gq"], p["gk"], cos, sin)
    attn = _attention(qt, k, vt, kn)
    x1, h2 = _out_proj(attn, u, x, mod, p["w_dw"], p["b_dw"], p["g_cn"], p["b_cn"], p["w_out"], p["g2"])
    return _ffn(h2, x1, mod, p["w_gate"], p["w_up"], p["w_down"], p["g_final"])


def kernel(x_prompt, x_sample, c_prompt, c_sample, w_ada, b_ada, g_norm1, w_in, b_in, g_q, g_k, w_dw, b_dw,
           g_cn, b_cn, w_out, g_norm2, w_gate, w_up, w_down, g_final):
    assert w_ada.shape[0] == 1, "single layer"
    assert x_prompt.shape[1] == x_sample.shape[1]
    seq = x_prompt.shape[1]
    assert seq % FF_ROW_TILE == 0 and seq % Q_TILE == 0 and seq % GRID_W == 0
    nb_p, nb_s = x_prompt.shape[0], x_sample.shape[0]

    c_all = jnp.concatenate([c_prompt, c_sample], axis=0)
    pad = (-c_all.shape[0]) % 8
    c_pad = jnp.pad(c_all, ((0, pad), (0, 0)))
    mod = _modulation(c_pad, w_ada[0], b_ada[0][None, :]).reshape(c_pad.shape[0], 6, D_MODEL)

    p = {
        "g1": g_norm1[0][None, :],
        "w_in": _reorder_qk_columns(w_in[0].astype(BF16)),
        "b_in": _reorder_qk_columns(b_in[0])[None, :],
        "gq": _reorder_head_dims(g_q[0])[None, :],
        "gk": _reorder_head_dims(g_k[0])[None, :],
        "w_dw": w_dw[0],
        "b_dw": b_dw[0][None, :],
        "g_cn": g_cn[0][None, :],
        "b_cn": b_cn[0][None, :],
        "w_out": w_out[0].astype(BF16),
        "g2": g_norm2[0][None, :],
        "w_gate": w_gate[0].astype(BF16),
        "w_up": w_up[0].astype(BF16),
        "w_down": w_down[0].astype(BF16),
        "g_final": g_final[None, :],
    }
    tabs = _rope_tables(seq)
    y_prompt = _layer(x_prompt, mod[:nb_p], tabs, p)
    y_sample = _layer(x_sample, mod[nb_p:nb_p + nb_s], tabs, p)
    return (y_prompt, y_sample)
```

```python
import functools
import math

import jax
import jax.numpy as jnp
from jax import lax
from jax.experimental import pallas as pl
from jax.experimental.pallas import tpu as pltpu

F32 = jnp.float32
BF16 = jnp.bfloat16

D_MODEL = 2048
HEAD_DIM = 128
N_HEADS = 8
N_KV_HEADS = 2
GROUP = N_HEADS // N_KV_HEADS
ATTN_DIM = N_HEADS * HEAD_DIM
KV_DIM = N_KV_HEADS * HEAD_DIM
CONV_DIM = D_MODEL - ATTN_DIM
CONV_WIDTH = 31
CONV_PAD = CONV_WIDTH // 2
IN_DIM = ATTN_DIM + 2 * KV_DIM + 2 * CONV_DIM
D_FF = 5632
GRID_W = 64
ROPE_THETA = 10000.0
EPS = 1e-6

LANES = 128
SUBLANES = 8
BF16_ROWS = 16
MXU_DEPTH = 256
V7X_VMEM_LIMIT_BYTES = 56 * 1024 * 1024
V7X_FFN_VMEM_LIMIT_BYTES = 60 * 1024 * 1024

ROW_TILE = 512
KV_CHUNK = 512
Q_TILE = 2048
FF_TILE = 512
FF_ROW_TILE = 1024
X1_PIECES = 8
X1_PIECE_ROWS = FF_ROW_TILE // X1_PIECES
HALO = 16
CONV_ROWS = 256
MOD_COLS = 1536

Q_SCALE = math.log2(math.e) / math.sqrt(HEAD_DIM)
SAFE_SHIFT = 48.0
NORM_MARGIN = 1.02


def _dot(a, b):
    return jnp.dot(a, b, preferred_element_type=F32)


def _params(*sem):
    return pltpu.CompilerParams(dimension_semantics=sem, vmem_limit_bytes=V7X_VMEM_LIMIT_BYTES)


def _const_spec(shape):
    nd = len(shape)
    return pl.BlockSpec(shape, lambda *_: (0,) * nd, pipeline_mode=pl.Buffered(1))


def _mod_kernel(c_ref, w_ref, b_ref, o_ref):
    c = c_ref[...]
    a = (c * jax.nn.sigmoid(c)).astype(BF16)
    o_ref[...] = _dot(a, w_ref[...].astype(BF16)) + b_ref[...]


def _modulation(c_pad, w_ada, b_ada):
    rows = c_pad.shape[0]
    n = w_ada.shape[1]
    return pl.pallas_call(
        _mod_kernel,
        grid=(n // MOD_COLS,),
        in_specs=[
            pl.BlockSpec((rows, D_MODEL), lambda j: (0, 0)),
            pl.BlockSpec((D_MODEL, MOD_COLS), lambda j: (0, j)),
            pl.BlockSpec((1, MOD_COLS), lambda j: (0, j)),
        ],
        out_specs=pl.BlockSpec((rows, MOD_COLS), lambda j: (0, j)),
        out_shape=jax.ShapeDtypeStruct((rows, n), F32),
        compiler_params=_params("arbitrary"),
        name="modulation",
    )(c_pad, w_ada, b_ada)


def _rope_kernel(cos_ref, sin_ref, tc_ref, ts_ref, *, n_rows):
    shape = tc_ref.shape
    r = lax.broadcasted_iota(jnp.int32, shape, 0)
    lane = lax.broadcasted_iota(jnp.int32, shape, 1)
    pos = jnp.where(r < n_rows, r, r - n_rows).astype(F32)
    freq = (lane % (HEAD_DIM // 4)).astype(F32)
    inv = jnp.exp(freq * (-2.0 / (HEAD_DIM // 2) * math.log(ROPE_THETA)))
    ang = pos * inv
    tc_ref[...] = jnp.cos(ang)
    ts_ref[...] = jnp.where(lane < HEAD_DIM // 2, -1.0, 1.0) * jnp.sin(ang)

    lane_b = lax.broadcasted_iota(jnp.int32, (GRID_W, LANES), 1)
    row_lane = (lane_b // (HEAD_DIM // 4)) % 2 == 0
    col_c = tc_ref[n_rows:, :]
    col_s = ts_ref[n_rows:, :]

    def body(g, carry):
        rows = pl.ds(pl.multiple_of(g * GRID_W, GRID_W), GRID_W)
        cos_ref[rows, :] = jnp.where(row_lane, tc_ref[pl.ds(g, 1), :], col_c)
        sin_ref[rows, :] = jnp.where(row_lane, ts_ref[pl.ds(g, 1), :], col_s)
        return carry

    lax.fori_loop(0, n_rows, body, 0)


def _rope_tables(seq):
    n_rows = seq // GRID_W
    table = jax.ShapeDtypeStruct((seq, LANES), F32)
    small = pltpu.VMEM((n_rows + GRID_W, LANES), F32)
    return pl.pallas_call(
        functools.partial(_rope_kernel, n_rows=n_rows),
        out_shape=(table, table),
        scratch_shapes=[small, small],
        compiler_params=pltpu.CompilerParams(vmem_limit_bytes=V7X_VMEM_LIMIT_BYTES),
        name="rope_table",
    )()


def _inproj_kernel(x_ref, mod_ref, g1_ref, w_ref, b_ref, gq_ref, gk_ref, cos_ref, sin_ref,
                   qt_ref, k_ref, vt_ref, u_ref, kn_ref):
    @pl.when(pl.program_id(1) == 0)
    def _():
        kn_ref[...] = jnp.zeros(kn_ref.shape, F32)

    x = x_ref[0]
    ms = jnp.mean(x * x, axis=-1, keepdims=True)
    sh1 = mod_ref[0, 0:1, :]
    sc1 = mod_ref[0, 1:2, :]
    h = (x * lax.rsqrt(ms + EPS) * g1_ref[...]) * (1.0 + sc1) + sh1
    hb = h.astype(BF16)
    cos = cos_ref[...]
    sin = sin_ref[...]

    def norm_rope(z, g):
        r = lax.rsqrt(jnp.mean(z * z, axis=-1, keepdims=True) + EPS)
        y = z * r * g
        return y * cos + pltpu.roll(y, HEAD_DIM // 2, 1) * sin

    k0 = ATTN_DIM
    v0 = ATTN_DIM + KV_DIM
    u0 = ATTN_DIM + 2 * KV_DIM
    zq = _dot(hb, w_ref[:, 0:k0]) + b_ref[:, 0:k0]
    zk = _dot(hb, w_ref[:, k0:v0]) + b_ref[:, k0:v0]
    zv = _dot(hb, w_ref[:, v0:u0]) + b_ref[:, v0:u0]
    za = _dot(hb, w_ref[:, u0:u0 + CONV_DIM]) + b_ref[:, u0:u0 + CONV_DIM]
    zg = _dot(hb, w_ref[:, u0 + CONV_DIM:]) + b_ref[:, u0 + CONV_DIM:]

    for hd in range(N_HEADS):
        sl = slice(hd * HEAD_DIM, (hd + 1) * HEAD_DIM)
        y = norm_rope(zq[:, sl], gq_ref[...]) * Q_SCALE
        qt_ref[0, sl, :] = y.T.astype(BF16)

    for hd in range(N_KV_HEADS):
        sl = slice(hd * HEAD_DIM, (hd + 1) * HEAD_DIM)
        kh = norm_rope(zk[:, sl], gk_ref[...])
        k_ref[0, :, sl] = kh.astype(BF16)
        n2 = jnp.max(jnp.sum(kh * kh, axis=-1, keepdims=True), axis=0, keepdims=True)
        kn_ref[0, hd:hd + 1, :] = jnp.maximum(kn_ref[0, hd:hd + 1, :], n2)

    vt_ref[0, 0] = zv.T.astype(BF16)

    glu = za * jax.nn.sigmoid(zg)
    for c in range(CONV_DIM // LANES):
        u_ref[0, c] = glu[:, c * LANES:(c + 1) * LANES]


def _in_proj(x, mod, g1, w_in, b_in, gq, gk, cos, sin):
    bsz, seq, _ = x.shape
    nt = seq // ROW_TILE
    return pl.pallas_call(
        _inproj_kernel,
        grid=(bsz, nt),
        in_specs=[
            pl.BlockSpec((1, ROW_TILE, D_MODEL), lambda b, i: (b, i, 0)),
            pl.BlockSpec((1, 6, D_MODEL), lambda b, i: (b, 0, 0)),
            _const_spec((1, D_MODEL)),
            _const_spec((D_MODEL, IN_DIM)),
            _const_spec((1, IN_DIM)),
            _const_spec((1, HEAD_DIM)),
            _const_spec((1, HEAD_DIM)),
            pl.BlockSpec((ROW_TILE, LANES), lambda b, i: (i, 0)),
            pl.BlockSpec((ROW_TILE, LANES), lambda b, i: (i, 0)),
        ],
        out_specs=[
            pl.BlockSpec((1, ATTN_DIM, ROW_TILE), lambda b, i: (b, 0, i)),
            pl.BlockSpec((1, ROW_TILE, KV_DIM), lambda b, i: (b, i, 0)),
            pl.BlockSpec((1, 1, KV_DIM, KV_CHUNK), lambda b, i: (b, i, 0, 0)),
            pl.BlockSpec((1, CONV_DIM // LANES, ROW_TILE, LANES), lambda b, i: (b, 0, i, 0)),
            pl.BlockSpec((1, SUBLANES, LANES), lambda b, i: (b, 0, 0)),
        ],
        out_shape=[
            jax.ShapeDtypeStruct((bsz, ATTN_DIM, seq), BF16),
            jax.ShapeDtypeStruct((bsz, seq, KV_DIM), BF16),
            jax.ShapeDtypeStruct((bsz, nt, KV_DIM, KV_CHUNK), BF16),
            jax.ShapeDtypeStruct((bsz, CONV_DIM // LANES, seq, LANES), F32),
            jax.ShapeDtypeStruct((bsz, SUBLANES, LANES), F32),
        ],
        compiler_params=_params("parallel", "arbitrary"),
        name="in_proj",
    )(x, mod, g1, w_in, b_in, gq, gk, cos, sin)


def _attn_kernel(qt_ref, k_ref, vt_ref, kn_ref, o_ref, qa_ref, p0_ref, p1_ref, m_ref, l_ref, acc_ref, *, n_chunks):
    h = pl.program_id(1)
    tq = qt_ref.shape[2]
    gw = GROUP * tq
    k_norm = jnp.sqrt(kn_ref[0, pl.ds(h, 1), :][:, 0:1])
    row = lax.broadcasted_iota(jnp.int32, (BF16_ROWS, tq), 0)
    worst = jnp.zeros((1, 1), F32)
    for g in range(GROUP):
        q = qt_ref[0, g * HEAD_DIM:(g + 1) * HEAD_DIM, :]
        qf = q.astype(F32)
        shift = jnp.sqrt(jnp.sum(qf * qf, axis=0, keepdims=True)) * k_norm * NORM_MARGIN
        worst = jnp.maximum(worst, jnp.max(shift, axis=1, keepdims=True))
        cs = slice(g * tq, (g + 1) * tq)
        qa_ref[0:HEAD_DIM, cs] = q
        qa_ref[HEAD_DIM:HEAD_DIM + BF16_ROWS, cs] = jnp.where(row == 0, -shift, 0.0).astype(BF16)
        qa_ref[HEAD_DIM + BF16_ROWS:, cs] = jnp.zeros((MXU_DEPTH - HEAD_DIM - BF16_ROWS, tq), BF16)
    safe = worst[0, 0] <= SAFE_SHIFT

    acc_ref[...] = jnp.zeros(acc_ref.shape, F32)

    @pl.when(safe)
    def _():
        lane = lax.broadcasted_iota(jnp.int32, (KV_CHUNK, LANES), 1)
        ones_col = jnp.where(lane == 0, 1.0, 0.0).astype(BF16)

        def qk_exp(c, p_ref):
            off = pl.multiple_of(c * KV_CHUNK, KV_CHUNK)
            ka = jnp.concatenate([k_ref[0, pl.ds(off, KV_CHUNK), :], ones_col], axis=1)
            p = jnp.exp2(_dot(ka, qa_ref[...]))
            p_ref[...] = p.astype(BF16)
            return jnp.sum(p.reshape(KV_CHUNK // SUBLANES, SUBLANES, gw), axis=0)

        def pv(c, p_ref):
            acc_ref[...] += _dot(vt_ref[0, c], p_ref[...])

        def body(i, l):
            c = 2 * i
            l = l + qk_exp(c + 1, p1_ref)
            pv(c, p0_ref)
            l = l + qk_exp(c + 2, p0_ref)
            pv(c + 1, p1_ref)
            return l

        l = lax.fori_loop(0, n_chunks // 2 - 1, body, qk_exp(0, p0_ref))
        l = l + qk_exp(n_chunks - 1, p1_ref)
        pv(n_chunks - 2, p0_ref)
        pv(n_chunks - 1, p1_ref)
        l_ref[...] = l

    @pl.when(jnp.logical_not(safe))
    def _():
        m_ref[...] = jnp.full(m_ref.shape, -jnp.inf, F32)
        l_ref[...] = jnp.zeros(l_ref.shape, F32)
        first_row = lax.broadcasted_iota(jnp.int32, (SUBLANES, tq), 0) == 0

        def body(c, carry):
            off = pl.multiple_of(c * KV_CHUNK, KV_CHUNK)
            kb = k_ref[0, pl.ds(off, KV_CHUNK), :]
            vb = vt_ref[0, c]
            for g in range(GROUP):
                cs = slice(g * tq, (g + 1) * tq)
                s = _dot(kb, qt_ref[0, g * HEAD_DIM:(g + 1) * HEAD_DIM, :])
                m_old = m_ref[:, cs]
                m_new = jnp.maximum(m_old, jnp.max(s, axis=0, keepdims=True))
                alpha = jnp.exp2(m_old - m_new)
                p = jnp.exp2(s - m_new)
                l_ref[:, cs] = alpha * l_ref[:, cs] + jnp.where(first_row, jnp.sum(p, axis=0, keepdims=True), 0.0)
                acc_ref[:, cs] = alpha * acc_ref[:, cs] + _dot(vb, p.astype(BF16))
                m_ref[:, cs] = m_new
            return carry

        lax.fori_loop(0, n_chunks, body, 0)

    o = acc_ref[...] * (1.0 / jnp.sum(l_ref[...], axis=0, keepdims=True))
    for g in range(GROUP):
        o_ref[0, :, g * HEAD_DIM:(g + 1) * HEAD_DIM] = o[:, g * tq:(g + 1) * tq].T.astype(BF16)


def _attention(qt, k, vt, kn):
    bsz, _, seq = qt.shape
    n_chunks = seq // KV_CHUNK
    assert n_chunks % 2 == 0 and n_chunks >= 4
    gw = GROUP * HEAD_DIM
    return pl.pallas_call(
        functools.partial(_attn_kernel, n_chunks=n_chunks),
        grid=(bsz, N_KV_HEADS, seq // Q_TILE),
        in_specs=[
            pl.BlockSpec((1, gw, Q_TILE), lambda b, h, i: (b, h, i)),
            pl.BlockSpec((1, seq, HEAD_DIM), lambda b, h, i: (b, 0, h)),
            pl.BlockSpec((1, n_chunks, HEAD_DIM, KV_CHUNK), lambda b, h, i: (b, 0, h, 0)),
            pl.BlockSpec((1, SUBLANES, LANES), lambda b, h, i: (b, 0, 0)),
        ],
        out_specs=pl.BlockSpec((1, Q_TILE, gw), lambda b, h, i: (b, i, h)),
        out_shape=jax.ShapeDtypeStruct((bsz, seq, ATTN_DIM), BF16),
        scratch_shapes=[
            pltpu.VMEM((MXU_DEPTH, GROUP * Q_TILE), BF16),
            pltpu.VMEM((KV_CHUNK, GROUP * Q_TILE), BF16),
            pltpu.VMEM((KV_CHUNK, GROUP * Q_TILE), BF16),
            pltpu.VMEM((1, GROUP * Q_TILE), F32),
            pltpu.VMEM((SUBLANES, GROUP * Q_TILE), F32),
            pltpu.VMEM((HEAD_DIM, GROUP * Q_TILE), F32),
        ],
        compiler_params=_params("parallel", "parallel", "arbitrary"),
        name="attention",
    )(qt, k, vt, kn)


def _outproj_kernel(attn_ref, u_ref, up_ref, un_ref, x_ref, mod_ref, wdw_ref, bdw_ref, gcn_ref, bcn_ref,
                    wout_ref, g2_ref, x1_ref, h2_ref, buf_ref):
    i = pl.program_id(1)
    last = pl.num_programs(1) - 1
    buf_ref[:, 0:HALO, :] = jnp.where(i > 0, up_ref[0], 0.0)
    buf_ref[:, HALO:HALO + ROW_TILE, :] = u_ref[0]
    buf_ref[:, HALO + ROW_TILE:, :] = jnp.where(i < last, un_ref[0], 0.0)

    gt1 = mod_ref[0, 2:3, :]
    sh2 = mod_ref[0, 3:4, :]
    g2s = g2_ref[...] * (1.0 + mod_ref[0, 4:5, :])
    base = HALO - CONV_PAD
    for r0 in range(0, ROW_TILE, CONV_ROWS):
        rows = slice(r0, r0 + CONV_ROWS)
        cols = []
        for c in range(CONV_DIM // LANES):
            cs = slice(c * LANES, (c + 1) * LANES)
            acc = jnp.broadcast_to(bdw_ref[:, cs], (CONV_ROWS, LANES))
            for j in range(CONV_WIDTH):
                lo = base + r0 + j
                acc = acc + buf_ref[c, lo:lo + CONV_ROWS, :] * wdw_ref[j:j + 1, cs]
            cols.append(acc)
        hc = jnp.concatenate(cols, axis=1)
        mu = jnp.mean(hc, axis=-1, keepdims=True)
        d = hc - mu
        var = jnp.mean(d * d, axis=-1, keepdims=True)
        y = d * lax.rsqrt(var + EPS) * gcn_ref[...] + bcn_ref[...]
        conv = (y * jax.nn.sigmoid(y)).astype(BF16)

        mix = _dot(attn_ref[0, rows, :], wout_ref[0:ATTN_DIM, :]) + _dot(conv, wout_ref[ATTN_DIM:, :])
        x1 = x_ref[0, rows, :] + gt1 * mix
        x1_ref[0, rows, :] = x1
        ms = jnp.mean(x1 * x1, axis=-1, keepdims=True)
        h2_ref[0, rows, :] = (x1 * lax.rsqrt(ms + EPS) * g2s + sh2).astype(BF16)


def _out_proj(attn, u, x, mod, w_dw, b_dw, g_cn, b_cn, w_out, g2):
    bsz, seq, _ = x.shape
    nt = seq // ROW_TILE
    hb = ROW_TILE // HALO
    n_halo = seq // HALO
    n_slab = CONV_DIM // LANES
    return pl.pallas_call(
        _outproj_kernel,
        grid=(bsz, nt),
        in_specs=[
            pl.BlockSpec((1, ROW_TILE, ATTN_DIM), lambda b, i: (b, i, 0)),
            pl.BlockSpec((1, n_slab, ROW_TILE, LANES), lambda b, i: (b, 0, i, 0)),
            pl.BlockSpec((1, n_slab, HALO, LANES), lambda b, i: (b, 0, jnp.maximum(i * hb - 1, 0), 0)),
            pl.BlockSpec((1, n_slab, HALO, LANES), lambda b, i: (b, 0, jnp.minimum((i + 1) * hb, n_halo - 1), 0)),
            pl.BlockSpec((1, ROW_TILE, D_MODEL), lambda b, i: (b, i, 0)),
            pl.BlockSpec((1, 6, D_MODEL), lambda b, i: (b, 0, 0)),
            _const_spec((CONV_WIDTH, CONV_DIM)),
            _const_spec((1, CONV_DIM)),
            _const_spec((1, CONV_DIM)),
            _const_spec((1, CONV_DIM)),
            _const_spec((D_MODEL, D_MODEL)),
            _const_spec((1, D_MODEL)),
        ],
        out_specs=[
            pl.BlockSpec((1, ROW_TILE, D_MODEL), lambda b, i: (b, i, 0)),
            pl.BlockSpec((1, ROW_TILE, D_MODEL), lambda b, i: (b, i, 0)),
        ],
        out_shape=[
            jax.ShapeDtypeStruct((bsz, seq, D_MODEL), F32),
            jax.ShapeDtypeStruct((bsz, seq, D_MODEL), BF16),
        ],
        scratch_shapes=[
            pltpu.VMEM((n_slab, ROW_TILE + 2 * HALO, LANES), F32),
        ],
        compiler_params=_params("parallel", "parallel"),
        name="out_proj",
    )(attn, u, u, u, x, mod, w_dw, b_dw, g_cn, b_cn, w_out, g2)


def _ffn_kernel(h2_ref, x1p_ref, mod_ref, wg_ref, wu_ref, wd_ref, gf_ref, y_ref, x1_ref):
    j = pl.program_id(2)

    @pl.when(j == 0)
    def _():
        y_ref[...] = jnp.zeros(y_ref.shape, F32)

    @pl.when(j < X1_PIECES)
    def _():
        x1_ref[pl.ds(pl.multiple_of(j * X1_PIECE_ROWS, X1_PIECE_ROWS), X1_PIECE_ROWS), :] = x1p_ref[0]

    h = h2_ref[0]
    g = _dot(h, wg_ref[...])
    u = _dot(h, wu_ref[...])
    a = (g * jax.nn.sigmoid(g) * u).astype(BF16)
    y_ref[0] += _dot(a, wd_ref[...])

    @pl.when(j == pl.num_programs(2) - 1)
    def _():
        gt2 = mod_ref[0, 5:6, :]

        def finish(r, carry):
            rows = pl.ds(pl.multiple_of(r * X1_PIECE_ROWS, X1_PIECE_ROWS), X1_PIECE_ROWS)
            x2 = x1_ref[rows, :] + gt2 * y_ref[0, rows, :]
            ms = jnp.mean(x2 * x2, axis=-1, keepdims=True)
            y_ref[0, rows, :] = x2 * lax.rsqrt(ms + EPS) * gf_ref[...]
            return carry

        lax.fori_loop(0, X1_PIECES, finish, 0)


def _ffn(h2, x1, mod, w_gate, w_up, w_down, g_final):
    bsz, seq, _ = x1.shape
    n_ff = D_FF // FF_TILE
    assert n_ff >= X1_PIECES
    return pl.pallas_call(
        _ffn_kernel,
        grid=(bsz, seq // FF_ROW_TILE, n_ff),
        in_specs=[
            pl.BlockSpec((1, FF_ROW_TILE, D_MODEL), lambda b, i, j: (b, i, 0)),
            pl.BlockSpec((1, X1_PIECE_ROWS, D_MODEL),
                         lambda b, i, j: (b, i * X1_PIECES + jnp.minimum(j, X1_PIECES - 1), 0)),
            pl.BlockSpec((1, 6, D_MODEL), lambda b, i, j: (b, 0, 0)),
            pl.BlockSpec((D_MODEL, FF_TILE), lambda b, i, j: (0, j)),
            pl.BlockSpec((D_MODEL, FF_TILE), lambda b, i, j: (0, j)),
            pl.BlockSpec((FF_TILE, D_MODEL), lambda b, i, j: (j, 0)),
            _const_spec((1, D_MODEL)),
        ],
        out_specs=pl.BlockSpec((1, FF_ROW_TILE, D_MODEL), lambda b, i, j: (b, i, 0)),
        out_shape=jax.ShapeDtypeStruct((bsz, seq, D_MODEL), F32),
        scratch_shapes=[pltpu.VMEM((FF_ROW_TILE, D_MODEL), F32)],
        compiler_params=pltpu.CompilerParams(dimension_semantics=("parallel", "parallel", "arbitrary"),
                                             vmem_limit_bytes=V7X_FFN_VMEM_LIMIT_BYTES),
        name="ffn",
    )(h2, x1, mod, w_gate, w_up, w_down, g_final)


def _reorder_head_dims(a):
    lead = a.shape[:-1]
    a = a.reshape(*lead, -1, 2, 2, HEAD_DIM // 4)
    return jnp.swapaxes(a, -3, -2).reshape(*lead, -1)


def _reorder_qk_columns(a):
    n_qk = (N_HEADS + N_KV_HEADS) * HEAD_DIM
    return jnp.concatenate([_reorder_head_dims(a[..., :n_qk]), a[..., n_qk:]], axis=-1)


def _layer(x, mod, tabs, p):
    cos, sin = tabs
    qt, k, vt, u, kn = _in_proj(x, mod, p["g1"], p["w_in"], p["b_in"], p["gq"], p["gk"], cos, sin)
    attn = _attention(qt, k, vt, kn)
    x1, h2 = _out_proj(attn, u, x, mod, p["w_dw"], p["b_dw"], p["g_cn"], p["b_cn"], p["w_out"], p["g2"])
    return _ffn(h2, x1, mod, p["w_gate"], p["w_up"], p["w_down"], p["g_final"])


def kernel(x_prompt, x_sample, c_prompt, c_sample, w_ada, b_ada, g_norm1, w_in, b_in, g_q, g_k, w_dw, b_dw,
           g_cn, b_cn, w_out, g_norm2, w_gate, w_up, w_down, g_final):
    assert w_ada.shape[0] == 1, "single layer"
    assert x_prompt.shape[1] == x_sample.shape[1]
    seq = x_prompt.shape[1]
    assert seq % FF_ROW_TILE == 0 and seq % Q_TILE == 0 and seq % GRID_W == 0
    nb_p, nb_s = x_prompt.shape[0], x_sample.shape[0]

    c_all = jnp.concatenate([c_prompt, c_sample], axis=0)
    pad = (-c_all.shape[0]) % 8
    c_pad = jnp.pad(c_all, ((0, pad), (0, 0)))
    mod = _modulation(c_pad, w_ada[0], b_ada[0][None, :]).reshape(c_pad.shape[0], 6, D_MODEL)

    p = {
        "g1": g_norm1[0][None, :],
        "w_in": _reorder_qk_columns(w_in[0].astype(BF16)),
        "b_in": _reorder_qk_columns(b_in[0])[None, :],
        "gq": _reorder_head_dims(g_q[0])[None, :],
        "gk": _reorder_head_dims(g_k[0])[None, :],
        "w_dw": w_dw[0],
        "b_dw": b_dw[0][None, :],
        "g_cn": g_cn[0][None, :],
        "b_cn": b_cn[0][None, :],
        "w_out": w_out[0].astype(BF16),
        "g2": g_norm2[0][None, :],
        "w_gate": w_gate[0].astype(BF16),
        "w_up": w_up[0].astype(BF16),
        "w_down": w_down[0].astype(BF16),
        "g_final": g_final[None, :],
    }
    tabs = _rope_tables(seq)
    y_prompt = _layer(x_prompt, mod[:nb_p], tabs, p)
    y_sample = _layer(x_sample, mod[nb_p:nb_p + nb_s], tabs, p)
    return (y_prompt, y_sample)
```

```python
import functools
import math

import jax
import jax.numpy as jnp
from jax import lax
from jax.experimental import pallas as pl
from jax.experimental.pallas import tpu as pltpu

F32 = jnp.float32
BF16 = jnp.bfloat16

D_MODEL = 2048
HEAD_DIM = 128
N_HEADS = 8
N_KV_HEADS = 2
GROUP = N_HEADS // N_KV_HEADS
ATTN_DIM = N_HEADS * HEAD_DIM
KV_DIM = N_KV_HEADS * HEAD_DIM
CONV_DIM = D_MODEL - ATTN_DIM
CONV_WIDTH = 31
CONV_PAD = CONV_WIDTH // 2
IN_DIM = ATTN_DIM + 2 * KV_DIM + 2 * CONV_DIM
D_FF = 5632
GRID_W = 64
ROPE_THETA = 10000.0
EPS = 1e-6
N_MOD = 6

LANES = 128
SUBLANES = 8
BF16_ROWS = 16
MXU_DEPTH = 256
V7X_VMEM_LIMIT_BYTES = 56 * 1024 * 1024

ROW_TILE = 512
KV_CHUNK = 512
Q_TILE = 2048
FF_TILE = 512
FF_ROW_TILE = 1024
X1_PIECES = 8
X1_PIECE_ROWS = FF_ROW_TILE // X1_PIECES
HALO = 16
CONV_ROWS = 256
MOD_COLS = 1536

Q_SCALE = math.log2(math.e) / math.sqrt(HEAD_DIM)
SAFE_SHIFT = 48.0
NORM_MARGIN = 1.02


def _dot(a, b):
    return jnp.dot(a, b, preferred_element_type=F32)


def _params(*sem):
    return pltpu.CompilerParams(dimension_semantics=sem, vmem_limit_bytes=V7X_VMEM_LIMIT_BYTES)


def _const_spec(shape):
    nd = len(shape)
    return pl.BlockSpec(shape, lambda *_: (0,) * nd, pipeline_mode=pl.Buffered(1))


def _mod_kernel(c_ref, w_ref, b_ref, o_ref):
    c = c_ref[...]
    a = (c * jax.nn.sigmoid(c)).astype(BF16)
    o_ref[...] = _dot(a, w_ref[...].astype(BF16)) + b_ref[...]


def _modulation(c_pad, w_ada, b_ada):
    rows = c_pad.shape[0]
    n = w_ada.shape[1]
    return pl.pallas_call(
        _mod_kernel,
        grid=(n // MOD_COLS,),
        in_specs=[
            pl.BlockSpec((rows, D_MODEL), lambda j: (0, 0)),
            pl.BlockSpec((D_MODEL, MOD_COLS), lambda j: (0, j)),
            pl.BlockSpec((1, MOD_COLS), lambda j: (0, j)),
        ],
        out_specs=pl.BlockSpec((rows, MOD_COLS), lambda j: (0, j)),
        out_shape=jax.ShapeDtypeStruct((rows, n), F32),
        compiler_params=_params("arbitrary"),
        name="modulation",
    )(c_pad, w_ada, b_ada)


def _rope_kernel(cos_ref, sin_ref, tc_ref, ts_ref, *, n_rows):
    shape = tc_ref.shape
    r = lax.broadcasted_iota(jnp.int32, shape, 0)
    lane = lax.broadcasted_iota(jnp.int32, shape, 1)
    pos = jnp.where(r < n_rows, r, r - n_rows).astype(F32)
    freq = (lane % (HEAD_DIM // 4)).astype(F32)
    inv = jnp.exp(freq * (-2.0 / (HEAD_DIM // 2) * math.log(ROPE_THETA)))
    ang = pos * inv
    tc_ref[...] = jnp.cos(ang)
    ts_ref[...] = jnp.where(lane < HEAD_DIM // 2, -1.0, 1.0) * jnp.sin(ang)

    lane_b = lax.broadcasted_iota(jnp.int32, (GRID_W, LANES), 1)
    row_lane = (lane_b // (HEAD_DIM // 4)) % 2 == 0
    col_c = tc_ref[n_rows:, :]
    col_s = ts_ref[n_rows:, :]

    def body(g, carry):
        rows = pl.ds(pl.multiple_of(g * GRID_W, GRID_W), GRID_W)
        cos_ref[rows, :] = jnp.where(row_lane, tc_ref[pl.ds(g, 1), :], col_c)
        sin_ref[rows, :] = jnp.where(row_lane, ts_ref[pl.ds(g, 1), :], col_s)
        return carry

    lax.fori_loop(0, n_rows, body, 0)


def _rope_tables(seq):
    n_rows = seq // GRID_W
    table = jax.ShapeDtypeStruct((seq, LANES), F32)
    small = pltpu.VMEM((n_rows + GRID_W, LANES), F32)
    return pl.pallas_call(
        functools.partial(_rope_kernel, n_rows=n_rows),
        out_shape=(table, table),
        scratch_shapes=[small, small],
        compiler_params=pltpu.CompilerParams(vmem_limit_bytes=V7X_VMEM_LIMIT_BYTES),
        name="rope_table",
    )()


def _inproj_kernel(x_ref, mod_ref, g1_ref, w_ref, b_ref, gq_ref, gk_ref, cos_ref, sin_ref,
                   qt_ref, k_ref, vt_ref, u_ref, kn_ref):
    @pl.when(pl.program_id(1) == 0)
    def _():
        kn_ref[...] = jnp.zeros(kn_ref.shape, F32)

    x = x_ref[0]
    ms = jnp.mean(x * x, axis=-1, keepdims=True)
    sh1 = mod_ref[0, 0:1, :]
    sc1 = mod_ref[0, 1:2, :]
    h = (x * lax.rsqrt(ms + EPS) * g1_ref[...]) * (1.0 + sc1) + sh1
    hb = h.astype(BF16)
    cos = cos_ref[...]
    sin = sin_ref[...]

    def norm_rope(z, g):
        r = lax.rsqrt(jnp.mean(z * z, axis=-1, keepdims=True) + EPS)
        y = z * r * g
        return y * cos + pltpu.roll(y, HEAD_DIM // 2, 1) * sin

    k0 = ATTN_DIM
    v0 = ATTN_DIM + KV_DIM
    u0 = ATTN_DIM + 2 * KV_DIM
    zq = _dot(hb, w_ref[:, 0:k0]) + b_ref[:, 0:k0]
    zk = _dot(hb, w_ref[:, k0:v0]) + b_ref[:, k0:v0]
    zv = _dot(hb, w_ref[:, v0:u0]) + b_ref[:, v0:u0]
    za = _dot(hb, w_ref[:, u0:u0 + CONV_DIM]) + b_ref[:, u0:u0 + CONV_DIM]
    zg = _dot(hb, w_ref[:, u0 + CONV_DIM:]) + b_ref[:, u0 + CONV_DIM:]

    for hd in range(N_HEADS):
        sl = slice(hd * HEAD_DIM, (hd + 1) * HEAD_DIM)
        y = norm_rope(zq[:, sl], gq_ref[...]) * Q_SCALE
        qt_ref[0, sl, :] = y.T.astype(BF16)

    for hd in range(N_KV_HEADS):
        sl = slice(hd * HEAD_DIM, (hd + 1) * HEAD_DIM)
        kh = norm_rope(zk[:, sl], gk_ref[...])
        k_ref[0, :, sl] = kh.astype(BF16)
        n2 = jnp.max(jnp.sum(kh * kh, axis=-1, keepdims=True), axis=0, keepdims=True)
        kn_ref[0, hd:hd + 1, :] = jnp.maximum(kn_ref[0, hd:hd + 1, :], n2)

    vt_ref[0, 0] = zv.T.astype(BF16)

    glu = za * jax.nn.sigmoid(zg)
    for c in range(CONV_DIM // LANES):
        u_ref[0, c] = glu[:, c * LANES:(c + 1) * LANES]


def _in_proj(x, mod, g1, w_in, b_in, gq, gk, cos, sin):
    bsz, seq, _ = x.shape
    nt = seq // ROW_TILE
    return pl.pallas_call(
        _inproj_kernel,
        grid=(bsz, nt),
        in_specs=[
            pl.BlockSpec((1, ROW_TILE, D_MODEL), lambda b, i: (b, i, 0)),
            pl.BlockSpec((1, N_MOD, D_MODEL), lambda b, i: (b, 0, 0)),
            _const_spec((1, D_MODEL)),
            _const_spec((D_MODEL, IN_DIM)),
            _const_spec((1, IN_DIM)),
            _const_spec((1, HEAD_DIM)),
            _const_spec((1, HEAD_DIM)),
            pl.BlockSpec((ROW_TILE, LANES), lambda b, i: (i, 0)),
            pl.BlockSpec((ROW_TILE, LANES), lambda b, i: (i, 0)),
        ],
        out_specs=[
            pl.BlockSpec((1, ATTN_DIM, ROW_TILE), lambda b, i: (b, 0, i)),
            pl.BlockSpec((1, ROW_TILE, KV_DIM), lambda b, i: (b, i, 0)),
            pl.BlockSpec((1, 1, KV_DIM, KV_CHUNK), lambda b, i: (b, i, 0, 0)),
            pl.BlockSpec((1, CONV_DIM // LANES, ROW_TILE, LANES), lambda b, i: (b, 0, i, 0)),
            pl.BlockSpec((1, SUBLANES, LANES), lambda b, i: (b, 0, 0)),
        ],
        out_shape=[
            jax.ShapeDtypeStruct((bsz, ATTN_DIM, seq), BF16),
            jax.ShapeDtypeStruct((bsz, seq, KV_DIM), BF16),
            jax.ShapeDtypeStruct((bsz, nt, KV_DIM, KV_CHUNK), BF16),
            jax.ShapeDtypeStruct((bsz, CONV_DIM // LANES, seq, LANES), F32),
            jax.ShapeDtypeStruct((bsz, SUBLANES, LANES), F32),
        ],
        compiler_params=_params("parallel", "arbitrary"),
        name="in_proj",
    )(x, mod, g1, w_in, b_in, gq, gk, cos, sin)


def _attn_kernel(qt_ref, k_ref, vt_ref, kn_ref, o_ref, qa_ref, p0_ref, p1_ref, m_ref, l_ref, acc_ref, *, n_chunks):
    h = pl.program_id(1)
    tq = qt_ref.shape[2]
    gw = GROUP * tq
    k_norm = jnp.sqrt(kn_ref[0, pl.ds(h, 1), :][:, 0:1])
    row = lax.broadcasted_iota(jnp.int32, (BF16_ROWS, tq), 0)
    worst = jnp.zeros((1, 1), F32)
    for g in range(GROUP):
        q = qt_ref[0, g * HEAD_DIM:(g + 1) * HEAD_DIM, :]
        qf = q.astype(F32)
        shift = jnp.sqrt(jnp.sum(qf * qf, axis=0, keepdims=True)) * k_norm * NORM_MARGIN
        worst = jnp.maximum(worst, jnp.max(shift, axis=1, keepdims=True))
        cs = slice(g * tq, (g + 1) * tq)
        qa_ref[0:HEAD_DIM, cs] = q
        qa_ref[HEAD_DIM:HEAD_DIM + BF16_ROWS, cs] = jnp.where(row == 0, -shift, 0.0).astype(BF16)
        qa_ref[HEAD_DIM + BF16_ROWS:, cs] = jnp.zeros((MXU_DEPTH - HEAD_DIM - BF16_ROWS, tq), BF16)
    safe = worst[0, 0] <= SAFE_SHIFT

    acc_ref[...] = jnp.zeros(acc_ref.shape, F32)

    @pl.when(safe)
    def _():
        lane = lax.broadcasted_iota(jnp.int32, (KV_CHUNK, LANES), 1)
        ones_col = jnp.where(lane == 0, 1.0, 0.0).astype(BF16)

        def qk_exp(c, p_ref):
            off = pl.multiple_of(c * KV_CHUNK, KV_CHUNK)
            ka = jnp.concatenate([k_ref[0, pl.ds(off, KV_CHUNK), :], ones_col], axis=1)
            p = jnp.exp2(_dot(ka, qa_ref[...]))
            p_ref[...] = p.astype(BF16)
            return jnp.sum(p.reshape(KV_CHUNK // SUBLANES, SUBLANES, gw), axis=0)

        def pv(c, p_ref):
            acc_ref[...] += _dot(vt_ref[0, c], p_ref[...])

        def body(i, l):
            c = 2 * i
            l = l + qk_exp(c + 1, p1_ref)
            pv(c, p0_ref)
            l = l + qk_exp(c + 2, p0_ref)
            pv(c + 1, p1_ref)
            return l

        l = lax.fori_loop(0, n_chunks // 2 - 1, body, qk_exp(0, p0_ref))
        l = l + qk_exp(n_chunks - 1, p1_ref)
        pv(n_chunks - 2, p0_ref)
        pv(n_chunks - 1, p1_ref)
        l_ref[...] = l

    @pl.when(jnp.logical_not(safe))
    def _():
        m_ref[...] = jnp.full(m_ref.shape, -jnp.inf, F32)
        l_ref[...] = jnp.zeros(l_ref.shape, F32)
        first_row = lax.broadcasted_iota(jnp.int32, (SUBLANES, tq), 0) == 0

        def body(c, carry):
            off = pl.multiple_of(c * KV_CHUNK, KV_CHUNK)
            kb = k_ref[0, pl.ds(off, KV_CHUNK), :]
            vb = vt_ref[0, c]
            for g in range(GROUP):
                cs = slice(g * tq, (g + 1) * tq)
                s = _dot(kb, qt_ref[0, g * HEAD_DIM:(g + 1) * HEAD_DIM, :])
                m_old = m_ref[:, cs]
                m_new = jnp.maximum(m_old, jnp.max(s, axis=0, keepdims=True))
                alpha = jnp.exp2(m_old - m_new)
                p = jnp.exp2(s - m_new)
                l_ref[:, cs] = alpha * l_ref[:, cs] + jnp.where(first_row, jnp.sum(p, axis=0, keepdims=True), 0.0)
                acc_ref[:, cs] = alpha * acc_ref[:, cs] + _dot(vb, p.astype(BF16))
                m_ref[:, cs] = m_new
            return carry

        lax.fori_loop(0, n_chunks, body, 0)

    o = acc_ref[...] * (1.0 / jnp.sum(l_ref[...], axis=0, keepdims=True))
    for g in range(GROUP):
        o_ref[0, :, g * HEAD_DIM:(g + 1) * HEAD_DIM] = o[:, g * tq:(g + 1) * tq].T.astype(BF16)


def _attention(qt, k, vt, kn):
    bsz, _, seq = qt.shape
    n_chunks = seq // KV_CHUNK
    assert n_chunks % 2 == 0 and n_chunks >= 4
    gw = GROUP * HEAD_DIM
    return pl.pallas_call(
        functools.partial(_attn_kernel, n_chunks=n_chunks),
        grid=(bsz, N_KV_HEADS, seq // Q_TILE),
        in_specs=[
            pl.BlockSpec((1, gw, Q_TILE), lambda b, h, i: (b, h, i)),
            pl.BlockSpec((1, seq, HEAD_DIM), lambda b, h, i: (b, 0, h)),
            pl.BlockSpec((1, n_chunks, HEAD_DIM, KV_CHUNK), lambda b, h, i: (b, 0, h, 0)),
            pl.BlockSpec((1, SUBLANES, LANES), lambda b, h, i: (b, 0, 0)),
        ],
        out_specs=pl.BlockSpec((1, Q_TILE, gw), lambda b, h, i: (b, i, h)),
        out_shape=jax.ShapeDtypeStruct((bsz, seq, ATTN_DIM), BF16),
        scratch_shapes=[
            pltpu.VMEM((MXU_DEPTH, GROUP * Q_TILE), BF16),
            pltpu.VMEM((KV_CHUNK, GROUP * Q_TILE), BF16),
            pltpu.VMEM((KV_CHUNK, GROUP * Q_TILE), BF16),
            pltpu.VMEM((1, GROUP * Q_TILE), F32),
            pltpu.VMEM((SUBLANES, GROUP * Q_TILE), F32),
            pltpu.VMEM((HEAD_DIM, GROUP * Q_TILE), F32),
        ],
        compiler_params=_params("parallel", "parallel", "arbitrary"),
        name="attention",
    )(qt, k, vt, kn)


def _outproj_kernel(attn_ref, u_ref, up_ref, un_ref, x_ref, mod_ref, wdw_ref, bdw_ref, gcn_ref, bcn_ref,
                    wout_ref, g2_ref, x1_ref, h2_ref, buf_ref):
    i = pl.program_id(1)
    last = pl.num_programs(1) - 1
    buf_ref[:, 0:HALO, :] = jnp.where(i > 0, up_ref[0], 0.0)
    buf_ref[:, HALO:HALO + ROW_TILE, :] = u_ref[0]
    buf_ref[:, HALO + ROW_TILE:, :] = jnp.where(i < last, un_ref[0], 0.0)

    gt1 = mod_ref[0, 2:3, :]
    sh2 = mod_ref[0, 3:4, :]
    g2s = g2_ref[...] * (1.0 + mod_ref[0, 4:5, :])
    base = HALO - CONV_PAD
    for r0 in range(0, ROW_TILE, CONV_ROWS):
        rows = slice(r0, r0 + CONV_ROWS)
        cols = []
        for c in range(CONV_DIM // LANES):
            cs = slice(c * LANES, (c + 1) * LANES)
            acc = jnp.broadcast_to(bdw_ref[:, cs], (CONV_ROWS, LANES))
            for j in range(CONV_WIDTH):
                lo = base + r0 + j
                acc = acc + buf_ref[c, lo:lo + CONV_ROWS, :] * wdw_ref[j:j + 1, cs]
            cols.append(acc)
        hc = jnp.concatenate(cols, axis=1)
        mu = jnp.mean(hc, axis=-1, keepdims=True)
        d = hc - mu
        var = jnp.mean(d * d, axis=-1, keepdims=True)
        y = d * lax.rsqrt(var + EPS) * gcn_ref[...] + bcn_ref[...]
        conv = (y * jax.nn.sigmoid(y)).astype(BF16)

        mix = _dot(jnp.concatenate([attn_ref[0, rows, :], conv], axis=1), wout_ref[...])
        x1 = x_ref[0, rows, :] + gt1 * mix
        x1_ref[0, rows, :] = x1
        ms = jnp.mean(x1 * x1, axis=-1, keepdims=True)
        h2_ref[0, rows, :] = (x1 * lax.rsqrt(ms + EPS) * g2s + sh2).astype(BF16)


def _out_proj(attn, u, x, mod, w_dw, b_dw, g_cn, b_cn, w_out, g2):
    bsz, seq, _ = x.shape
    nt = seq // ROW_TILE
    hb = ROW_TILE // HALO
    n_halo = seq // HALO
    n_slab = CONV_DIM // LANES
    return pl.pallas_call(
        _outproj_kernel,
        grid=(bsz, nt),
        in_specs=[
            pl.BlockSpec((1, ROW_TILE, ATTN_DIM), lambda b, i: (b, i, 0)),
            pl.BlockSpec((1, n_slab, ROW_TILE, LANES), lambda b, i: (b, 0, i, 0)),
            pl.BlockSpec((1, n_slab, HALO, LANES), lambda b, i: (b, 0, jnp.maximum(i * hb - 1, 0), 0)),
            pl.BlockSpec((1, n_slab, HALO, LANES), lambda b, i: (b, 0, jnp.minimum((i + 1) * hb, n_halo - 1), 0)),
            pl.BlockSpec((1, ROW_TILE, D_MODEL), lambda b, i: (b, i, 0)),
            pl.BlockSpec((1, N_MOD, D_MODEL), lambda b, i: (b, 0, 0)),
            _const_spec((CONV_WIDTH, CONV_DIM)),
            _const_spec((1, CONV_DIM)),
            _const_spec((1, CONV_DIM)),
            _const_spec((1, CONV_DIM)),
            _const_spec((D_MODEL, D_MODEL)),
            _const_spec((1, D_MODEL)),
        ],
        out_specs=[
            pl.BlockSpec((1, ROW_TILE, D_MODEL), lambda b, i: (b, i, 0)),
            pl.BlockSpec((1, ROW_TILE, D_MODEL), lambda b, i: (b, i, 0)),
        ],
        out_shape=[
            jax.ShapeDtypeStruct((bsz, seq, D_MODEL), F32),
            jax.ShapeDtypeStruct((bsz, seq, D_MODEL), BF16),
        ],
        scratch_shapes=[
            pltpu.VMEM((n_slab, ROW_TILE + 2 * HALO, LANES), F32),
        ],
        compiler_params=_params("parallel", "parallel"),
        name="out_proj",
    )(attn, u, u, u, x, mod, w_dw, b_dw, g_cn, b_cn, w_out, g2)


def _ffn_kernel(h2_ref, x1p_ref, mod_ref, wg_ref, wu_ref, wd_ref, gf_ref, y_ref, x1_ref):
    j = pl.program_id(2)

    @pl.when(j == 0)
    def _():
        y_ref[...] = jnp.zeros(y_ref.shape, F32)

    @pl.when(j < X1_PIECES)
    def _():
        x1_ref[pl.ds(pl.multiple_of(j * X1_PIECE_ROWS, X1_PIECE_ROWS), X1_PIECE_ROWS), :] = x1p_ref[0]

    h = h2_ref[0]
    g = _dot(h, wg_ref[...])
    u = _dot(h, wu_ref[...])
    a = (g * jax.nn.sigmoid(g) * u).astype(BF16)
    y_ref[0] += _dot(a, wd_ref[...])

    @pl.when(j == pl.num_programs(2) - 1)
    def _():
        gt2 = mod_ref[0, 5:6, :]

        def finish(r, carry):
            rows = pl.ds(pl.multiple_of(r * X1_PIECE_ROWS, X1_PIECE_ROWS), X1_PIECE_ROWS)
            x2 = x1_ref[rows, :] + gt2 * y_ref[0, rows, :]
            ms = jnp.mean(x2 * x2, axis=-1, keepdims=True)
            y_ref[0, rows, :] = x2 * lax.rsqrt(ms + EPS) * gf_ref[...]
            return carry

        lax.fori_loop(0, X1_PIECES, finish, 0)


def _ffn(h2, x1, mod, w_gate, w_up, w_down, g_final):
    bsz, seq, _ = x1.shape
    n_ff = D_FF // FF_TILE
    assert n_ff >= X1_PIECES
    return pl.pallas_call(
        _ffn_kernel,
        grid=(bsz, seq // FF_ROW_TILE, n_ff),
        in_specs=[
            pl.BlockSpec((1, FF_ROW_TILE, D_MODEL), lambda b, i, j: (b, i, 0)),
            pl.BlockSpec((1, X1_PIECE_ROWS, D_MODEL),
                         lambda b, i, j: (b, i * X1_PIECES + jnp.minimum(j, X1_PIECES - 1), 0)),
            pl.BlockSpec((1, N_MOD, D_MODEL), lambda b, i, j: (b, 0, 0)),
            pl.BlockSpec((D_MODEL, FF_TILE), lambda b, i, j: (0, j)),
            pl.BlockSpec((D_MODEL, FF_TILE), lambda b, i, j: (0, j)),
            pl.BlockSpec((FF_TILE, D_MODEL), lambda b, i, j: (j, 0)),
            _const_spec((1, D_MODEL)),
        ],
        out_specs=pl.BlockSpec((1, FF_ROW_TILE, D_MODEL), lambda b, i, j: (b, i, 0)),
        out_shape=jax.ShapeDtypeStruct((bsz, seq, D_MODEL), F32),
        scratch_shapes=[pltpu.VMEM((FF_ROW_TILE, D_MODEL), F32)],
        compiler_params=_params("parallel", "parallel", "arbitrary"),
        name="ffn",
    )(h2, x1, mod, w_gate, w_up, w_down, g_final)


def _reorder_head_dims(a):
    lead = a.shape[:-1]
    a = a.reshape(*lead, -1, 2, 2, HEAD_DIM // 4)
    return jnp.swapaxes(a, -3, -2).reshape(*lead, -1)


def _reorder_qk_columns(a):
    n_qk = (N_HEADS + N_KV_HEADS) * HEAD_DIM
    return jnp.concatenate([_reorder_head_dims(a[..., :n_qk]), a[..., n_qk:]], axis=-1)


def _layer(x, mod, tabs, p):
    cos, sin = tabs
    qt, k, vt, u, kn = _in_proj(x, mod, p["g1"], p["w_in"], p["b_in"], p["gq"], p["gk"], cos, sin)
    attn = _attention(qt, k, vt, kn)
    x1, h2 = _out_proj(attn, u, x, mod, p["w_dw"], p["b_dw"], p["g_cn"], p["b_cn"], p["w_out"], p["g2"])
    return _ffn(h2, x1, mod, p["w_gate"], p["w_up"], p["w_down"], p["g_final"])


def kernel(x_prompt, x_sample, c_prompt, c_sample, w_ada, b_ada, g_norm1, w_in, b_in, g_q, g_k, w_dw, b_dw,
           g_cn, b_cn, w_out, g_norm2, w_gate, w_up, w_down, g_final):
    assert w_ada.shape[0] == 1, "single layer"
    assert x_prompt.shape[1] == x_sample.shape[1]
    seq = x_prompt.shape[1]
    assert seq % FF_ROW_TILE == 0 and seq % Q_TILE == 0 and seq % GRID_W == 0
    nb_p, nb_s = x_prompt.shape[0], x_sample.shape[0]

    c_all = jnp.concatenate([c_prompt, c_sample], axis=0)
    pad = (-c_all.shape[0]) % SUBLANES
    c_pad = jnp.pad(c_all, ((0, pad), (0, 0)))
    mod = _modulation(c_pad, w_ada[0], b_ada[0][None, :]).reshape(c_pad.shape[0], N_MOD, D_MODEL)

    p = {
        "g1": g_norm1[0][None, :],
        "w_in": _reorder_qk_columns(w_in[0].astype(BF16)),
        "b_in": _reorder_qk_columns(b_in[0])[None, :],
        "gq": _reorder_head_dims(g_q[0])[None, :],
        "gk": _reorder_head_dims(g_k[0])[None, :],
        "w_dw": w_dw[0],
        "b_dw": b_dw[0][None, :],
        "g_cn": g_cn[0][None, :],
        "b_cn": b_cn[0][None, :],
        "w_out": w_out[0].astype(BF16),
        "g2": g_norm2[0][None, :],
        "w_gate": w_gate[0].astype(BF16),
        "w_up": w_up[0].astype(BF16),
        "w_down": w_down[0].astype(BF16),
        "g_final": g_final[None, :],
    }
    tabs = _rope_tables(seq)
    y_prompt = _layer(x_prompt, mod[:nb_p], tabs, p)
    y_sample = _layer(x_sample, mod[nb_p:nb_p + nb_s], tabs, p)
    return (y_prompt, y_sample)
```

```python
import functools
import math

import jax
import jax.numpy as jnp
from jax import lax
from jax.experimental import pallas as pl
from jax.experimental.pallas import tpu as pltpu

F32 = jnp.float32
BF16 = jnp.bfloat16

D_MODEL = 2048
HEAD_DIM = 128
N_HEADS = 8
N_KV_HEADS = 2
GROUP = N_HEADS // N_KV_HEADS
ATTN_DIM = N_HEADS * HEAD_DIM
KV_DIM = N_KV_HEADS * HEAD_DIM
CONV_DIM = D_MODEL - ATTN_DIM
CONV_WIDTH = 31
CONV_PAD = CONV_WIDTH // 2
IN_DIM = ATTN_DIM + 2 * KV_DIM + 2 * CONV_DIM
D_FF = 5632
GRID_W = 64
ROPE_THETA = 10000.0
EPS = 1e-6
N_MOD = 6

LANES = 128
SUBLANES = 8
BF16_ROWS = 16
MXU_DEPTH = 256
V7X_VMEM_LIMIT_BYTES = 56 * 1024 * 1024

ROW_TILE = 512
KV_CHUNK = 512
Q_TILE = 2048
FF_TILE = 512
FF_ROW_TILE = 1024
X1_PIECES = 8
X1_PIECE_ROWS = FF_ROW_TILE // X1_PIECES
HALO = 16
CONV_ROWS = 256
MOD_COLS = 1536

Q_SCALE = math.log2(math.e) / math.sqrt(HEAD_DIM)
SAFE_SHIFT = 48.0
NORM_MARGIN = 1.02


def _dot(a, b):
    return jnp.dot(a, b, preferred_element_type=F32)


def _params(*sem):
    return pltpu.CompilerParams(dimension_semantics=sem, vmem_limit_bytes=V7X_VMEM_LIMIT_BYTES)


def _const_spec(shape):
    nd = len(shape)
    return pl.BlockSpec(shape, lambda *_: (0,) * nd, pipeline_mode=pl.Buffered(1))


def _mod_kernel(c_ref, w_ref, b_ref, o_ref):
    c = c_ref[...]
    a = (c * jax.nn.sigmoid(c)).astype(BF16)
    o_ref[...] = _dot(a, w_ref[...].astype(BF16)) + b_ref[...]


def _modulation(c_pad, w_ada, b_ada):
    rows = c_pad.shape[0]
    n = w_ada.shape[1]
    return pl.pallas_call(
        _mod_kernel,
        grid=(n // MOD_COLS,),
        in_specs=[
            pl.BlockSpec((rows, D_MODEL), lambda j: (0, 0)),
            pl.BlockSpec((D_MODEL, MOD_COLS), lambda j: (0, j)),
            pl.BlockSpec((1, MOD_COLS), lambda j: (0, j)),
        ],
        out_specs=pl.BlockSpec((rows, MOD_COLS), lambda j: (0, j)),
        out_shape=jax.ShapeDtypeStruct((rows, n), F32),
        compiler_params=_params("arbitrary"),
        name="modulation",
    )(c_pad, w_ada, b_ada)


def _rope_kernel(cos_ref, sin_ref, tc_ref, ts_ref, *, n_rows):
    shape = tc_ref.shape
    r = lax.broadcasted_iota(jnp.int32, shape, 0)
    lane = lax.broadcasted_iota(jnp.int32, shape, 1)
    pos = jnp.where(r < n_rows, r, r - n_rows).astype(F32)
    freq = (lane % (HEAD_DIM // 4)).astype(F32)
    inv = jnp.exp(freq * (-2.0 / (HEAD_DIM // 2) * math.log(ROPE_THETA)))
    ang = pos * inv
    tc_ref[...] = jnp.cos(ang)
    ts_ref[...] = jnp.where(lane < HEAD_DIM // 2, -1.0, 1.0) * jnp.sin(ang)

    lane_b = lax.broadcasted_iota(jnp.int32, (GRID_W, LANES), 1)
    row_lane = (lane_b // (HEAD_DIM // 4)) % 2 == 0
    col_c = tc_ref[n_rows:, :]
    col_s = ts_ref[n_rows:, :]

    def body(g, carry):
        rows = pl.ds(pl.multiple_of(g * GRID_W, GRID_W), GRID_W)
        cos_ref[rows, :] = jnp.where(row_lane, tc_ref[pl.ds(g, 1), :], col_c)
        sin_ref[rows, :] = jnp.where(row_lane, ts_ref[pl.ds(g, 1), :], col_s)
        return carry

    lax.fori_loop(0, n_rows, body, 0)


def _rope_tables(seq):
    n_rows = seq // GRID_W
    table = jax.ShapeDtypeStruct((seq, LANES), F32)
    small = pltpu.VMEM((n_rows + GRID_W, LANES), F32)
    return pl.pallas_call(
        functools.partial(_rope_kernel, n_rows=n_rows),
        out_shape=(table, table),
        scratch_shapes=[small, small],
        compiler_params=pltpu.CompilerParams(vmem_limit_bytes=V7X_VMEM_LIMIT_BYTES),
        name="rope_table",
    )()


def _inproj_kernel(x_ref, mod_ref, g1_ref, w_ref, b_ref, gq_ref, gk_ref, cos_ref, sin_ref,
                   qt_ref, k_ref, vt_ref, u_ref, kn_ref):
    @pl.when(pl.program_id(1) == 0)
    def _():
        kn_ref[...] = jnp.zeros(kn_ref.shape, F32)

    x = x_ref[0]
    ms = jnp.mean(x * x, axis=-1, keepdims=True)
    sh1 = mod_ref[0, 0:1, :]
    sc1 = mod_ref[0, 1:2, :]
    h = (x * lax.rsqrt(ms + EPS) * g1_ref[...]) * (1.0 + sc1) + sh1
    hb = h.astype(BF16)
    cos = cos_ref[...]
    sin = sin_ref[...]

    def norm_rope(z, g):
        r = lax.rsqrt(jnp.mean(z * z, axis=-1, keepdims=True) + EPS)
        y = z * r * g
        return y * cos + pltpu.roll(y, HEAD_DIM // 2, 1) * sin

    k0 = ATTN_DIM
    v0 = ATTN_DIM + KV_DIM
    u0 = ATTN_DIM + 2 * KV_DIM
    zq = _dot(hb, w_ref[:, 0:k0]) + b_ref[:, 0:k0]
    zk = _dot(hb, w_ref[:, k0:v0]) + b_ref[:, k0:v0]
    zv = _dot(hb, w_ref[:, v0:u0]) + b_ref[:, v0:u0]
    za = _dot(hb, w_ref[:, u0:u0 + CONV_DIM]) + b_ref[:, u0:u0 + CONV_DIM]
    zg = _dot(hb, w_ref[:, u0 + CONV_DIM:]) + b_ref[:, u0 + CONV_DIM:]

    for hd in range(N_HEADS):
        sl = slice(hd * HEAD_DIM, (hd + 1) * HEAD_DIM)
        y = norm_rope(zq[:, sl], gq_ref[...]) * Q_SCALE
        qt_ref[0, sl, :] = y.T.astype(BF16)

    for hd in range(N_KV_HEADS):
        sl = slice(hd * HEAD_DIM, (hd + 1) * HEAD_DIM)
        kh = norm_rope(zk[:, sl], gk_ref[...])
        k_ref[0, :, sl] = kh.astype(BF16)
        n2 = jnp.max(jnp.sum(kh * kh, axis=-1, keepdims=True), axis=0, keepdims=True)
        kn_ref[0, hd:hd + 1, :] = jnp.maximum(kn_ref[0, hd:hd + 1, :], n2)

    vt_ref[0, 0] = zv.T.astype(BF16)

    glu = za * jax.nn.sigmoid(zg)
    for c in range(CONV_DIM // LANES):
        u_ref[0, c] = glu[:, c * LANES:(c + 1) * LANES]


def _in_proj(x, mod, g1, w_in, b_in, gq, gk, cos, sin):
    bsz, seq, _ = x.shape
    nt = seq // ROW_TILE
    return pl.pallas_call(
        _inproj_kernel,
        grid=(bsz, nt),
        in_specs=[
            pl.BlockSpec((1, ROW_TILE, D_MODEL), lambda b, i: (b, i, 0)),
            pl.BlockSpec((1, N_MOD, D_MODEL), lambda b, i: (b, 0, 0)),
            _const_spec((1, D_MODEL)),
            _const_spec((D_MODEL, IN_DIM)),
            _const_spec((1, IN_DIM)),
            _const_spec((1, HEAD_DIM)),
            _const_spec((1, HEAD_DIM)),
            pl.BlockSpec((ROW_TILE, LANES), lambda b, i: (i, 0)),
            pl.BlockSpec((ROW_TILE, LANES), lambda b, i: (i, 0)),
        ],
        out_specs=[
            pl.BlockSpec((1, ATTN_DIM, ROW_TILE), lambda b, i: (b, 0, i)),
            pl.BlockSpec((1, ROW_TILE, KV_DIM), lambda b, i: (b, i, 0)),
            pl.BlockSpec((1, 1, KV_DIM, KV_CHUNK), lambda b, i: (b, i, 0, 0)),
            pl.BlockSpec((1, CONV_DIM // LANES, ROW_TILE, LANES), lambda b, i: (b, 0, i, 0)),
            pl.BlockSpec((1, SUBLANES, LANES), lambda b, i: (b, 0, 0)),
        ],
        out_shape=[
            jax.ShapeDtypeStruct((bsz, ATTN_DIM, seq), BF16),
            jax.ShapeDtypeStruct((bsz, seq, KV_DIM), BF16),
            jax.ShapeDtypeStruct((bsz, nt, KV_DIM, KV_CHUNK), BF16),
            jax.ShapeDtypeStruct((bsz, CONV_DIM // LANES, seq, LANES), F32),
            jax.ShapeDtypeStruct((bsz, SUBLANES, LANES), F32),
        ],
        compiler_params=_params("parallel", "arbitrary"),
        name="in_proj",
    )(x, mod, g1, w_in, b_in, gq, gk, cos, sin)


def _attn_kernel(qt_ref, k_ref, vt_ref, kn_ref, o_ref, qa_ref, p0_ref, p1_ref, m_ref, l_ref, acc_ref, *, n_chunks):
    h = pl.program_id(1)
    tq = qt_ref.shape[2]
    gw = GROUP * tq
    k_norm = jnp.sqrt(kn_ref[0, pl.ds(h, 1), :][:, 0:1])
    row = lax.broadcasted_iota(jnp.int32, (BF16_ROWS, tq), 0)
    worst = jnp.zeros((1, 1), F32)
    for g in range(GROUP):
        q = qt_ref[0, g * HEAD_DIM:(g + 1) * HEAD_DIM, :]
        qf = q.astype(F32)
        shift = jnp.sqrt(jnp.sum(qf * qf, axis=0, keepdims=True)) * k_norm * NORM_MARGIN
        worst = jnp.maximum(worst, jnp.max(shift, axis=1, keepdims=True))
        cs = slice(g * tq, (g + 1) * tq)
        qa_ref[0:HEAD_DIM, cs] = q
        qa_ref[HEAD_DIM:HEAD_DIM + BF16_ROWS, cs] = jnp.where(row == 0, -shift, 0.0).astype(BF16)
        qa_ref[HEAD_DIM + BF16_ROWS:, cs] = jnp.zeros((MXU_DEPTH - HEAD_DIM - BF16_ROWS, tq), BF16)
    safe = worst[0, 0] <= SAFE_SHIFT

    acc_ref[...] = jnp.zeros(acc_ref.shape, F32)

    @pl.when(safe)
    def _():
        lane = lax.broadcasted_iota(jnp.int32, (KV_CHUNK, LANES), 1)
        ones_col = jnp.where(lane == 0, 1.0, 0.0).astype(BF16)

        def qk_exp(c, p_ref):
            off = pl.multiple_of(c * KV_CHUNK, KV_CHUNK)
            ka = jnp.concatenate([k_ref[0, pl.ds(off, KV_CHUNK), :], ones_col], axis=1)
            p = jnp.exp2(_dot(ka, qa_ref[...]))
            p_ref[...] = p.astype(BF16)
            return jnp.sum(p.reshape(KV_CHUNK // SUBLANES, SUBLANES, gw), axis=0)

        def pv(c, p_ref):
            acc_ref[...] += _dot(vt_ref[0, c], p_ref[...])

        def step(c, l):
            l = l + qk_exp(c + 1, p1_ref)
            pv(c, p0_ref)
            l = l + qk_exp(c + 2, p0_ref)
            pv(c + 1, p1_ref)
            return l

        def body(i, l):
            return step(4 * i + 2, step(4 * i, l))

        trips = (n_chunks - 4) // 4
        l = lax.fori_loop(0, trips, body, qk_exp(0, p0_ref))
        l = step(4 * trips, l)
        l = l + qk_exp(n_chunks - 1, p1_ref)
        pv(n_chunks - 2, p0_ref)
        pv(n_chunks - 1, p1_ref)
        l_ref[...] = l

    @pl.when(jnp.logical_not(safe))
    def _():
        m_ref[...] = jnp.full(m_ref.shape, -jnp.inf, F32)
        l_ref[...] = jnp.zeros(l_ref.shape, F32)
        first_row = lax.broadcasted_iota(jnp.int32, (SUBLANES, tq), 0) == 0

        def body(c, carry):
            off = pl.multiple_of(c * KV_CHUNK, KV_CHUNK)
            kb = k_ref[0, pl.ds(off, KV_CHUNK), :]
            vb = vt_ref[0, c]
            for g in range(GROUP):
                cs = slice(g * tq, (g + 1) * tq)
                s = _dot(kb, qt_ref[0, g * HEAD_DIM:(g + 1) * HEAD_DIM, :])
                m_old = m_ref[:, cs]
                m_new = jnp.maximum(m_old, jnp.max(s, axis=0, keepdims=True))
                alpha = jnp.exp2(m_old - m_new)
                p = jnp.exp2(s - m_new)
                l_ref[:, cs] = alpha * l_ref[:, cs] + jnp.where(first_row, jnp.sum(p, axis=0, keepdims=True), 0.0)
                acc_ref[:, cs] = alpha * acc_ref[:, cs] + _dot(vb, p.astype(BF16))
                m_ref[:, cs] = m_new
            return carry

        lax.fori_loop(0, n_chunks, body, 0)

    o = acc_ref[...] * (1.0 / jnp.sum(l_ref[...], axis=0, keepdims=True))
    for g in range(GROUP):
        o_ref[0, :, g * HEAD_DIM:(g + 1) * HEAD_DIM] = o[:, g * tq:(g + 1) * tq].T.astype(BF16)


def _attention(qt, k, vt, kn):
    bsz, _, seq = qt.shape
    n_chunks = seq // KV_CHUNK
    assert n_chunks % 4 == 0
    gw = GROUP * HEAD_DIM
    return pl.pallas_call(
        functools.partial(_attn_kernel, n_chunks=n_chunks),
        grid=(bsz, N_KV_HEADS, seq // Q_TILE),
        in_specs=[
            pl.BlockSpec((1, gw, Q_TILE), lambda b, h, i: (b, h, i)),
            pl.BlockSpec((1, seq, HEAD_DIM), lambda b, h, i: (b, 0, h)),
            pl.BlockSpec((1, n_chunks, HEAD_DIM, KV_CHUNK), lambda b, h, i: (b, 0, h, 0)),
            pl.BlockSpec((1, SUBLANES, LANES), lambda b, h, i: (b, 0, 0)),
        ],
        out_specs=pl.BlockSpec((1, Q_TILE, gw), lambda b, h, i: (b, i, h)),
        out_shape=jax.ShapeDtypeStruct((bsz, seq, ATTN_DIM), BF16),
        scratch_shapes=[
            pltpu.VMEM((MXU_DEPTH, GROUP * Q_TILE), BF16),
            pltpu.VMEM((KV_CHUNK, GROUP * Q_TILE), BF16),
            pltpu.VMEM((KV_CHUNK, GROUP * Q_TILE), BF16),
            pltpu.VMEM((1, GROUP * Q_TILE), F32),
            pltpu.VMEM((SUBLANES, GROUP * Q_TILE), F32),
            pltpu.VMEM((HEAD_DIM, GROUP * Q_TILE), F32),
        ],
        compiler_params=_params("parallel", "parallel", "arbitrary"),
        name="attention",
    )(qt, k, vt, kn)


def _outproj_kernel(attn_ref, u_ref, up_ref, un_ref, x_ref, mod_ref, wdw_ref, bdw_ref, gcn_ref, bcn_ref,
                    wout_ref, g2_ref, x1_ref, h2_ref, buf_ref):
    i = pl.program_id(1)
    last = pl.num_programs(1) - 1
    buf_ref[:, 0:HALO, :] = jnp.where(i > 0, up_ref[0], 0.0)
    buf_ref[:, HALO:HALO + ROW_TILE, :] = u_ref[0]
    buf_ref[:, HALO + ROW_TILE:, :] = jnp.where(i < last, un_ref[0], 0.0)

    gt1 = mod_ref[0, 2:3, :]
    sh2 = mod_ref[0, 3:4, :]
    g2s = g2_ref[...] * (1.0 + mod_ref[0, 4:5, :])
    base = HALO - CONV_PAD
    for r0 in range(0, ROW_TILE, CONV_ROWS):
        rows = slice(r0, r0 + CONV_ROWS)
        cols = []
        for c in range(CONV_DIM // LANES):
            cs = slice(c * LANES, (c + 1) * LANES)
            acc = jnp.broadcast_to(bdw_ref[:, cs], (CONV_ROWS, LANES))
            for j in range(CONV_WIDTH):
                lo = base + r0 + j
                acc = acc + buf_ref[c, lo:lo + CONV_ROWS, :] * wdw_ref[j:j + 1, cs]
            cols.append(acc)
        hc = jnp.concatenate(cols, axis=1)
        mu = jnp.mean(hc, axis=-1, keepdims=True)
        d = hc - mu
        var = jnp.mean(d * d, axis=-1, keepdims=True)
        y = d * lax.rsqrt(var + EPS) * gcn_ref[...] + bcn_ref[...]
        conv = (y * jax.nn.sigmoid(y)).astype(BF16)

        mix = _dot(jnp.concatenate([attn_ref[0, rows, :], conv], axis=1), wout_ref[...])
        x1 = x_ref[0, rows, :] + gt1 * mix
        x1_ref[0, rows, :] = x1
        ms = jnp.mean(x1 * x1, axis=-1, keepdims=True)
        h2_ref[0, rows, :] = (x1 * lax.rsqrt(ms + EPS) * g2s + sh2).astype(BF16)


def _out_proj(attn, u, x, mod, w_dw, b_dw, g_cn, b_cn, w_out, g2):
    bsz, seq, _ = x.shape
    nt = seq // ROW_TILE
    hb = ROW_TILE // HALO
    n_halo = seq // HALO
    n_slab = CONV_DIM // LANES
    return pl.pallas_call(
        _outproj_kernel,
        grid=(bsz, nt),
        in_specs=[
            pl.BlockSpec((1, ROW_TILE, ATTN_DIM), lambda b, i: (b, i, 0)),
            pl.BlockSpec((1, n_slab, ROW_TILE, LANES), lambda b, i: (b, 0, i, 0)),
            pl.BlockSpec((1, n_slab, HALO, LANES), lambda b, i: (b, 0, jnp.maximum(i * hb - 1, 0), 0)),
            pl.BlockSpec((1, n_slab, HALO, LANES), lambda b, i: (b, 0, jnp.minimum((i + 1) * hb, n_halo - 1), 0)),
            pl.BlockSpec((1, ROW_TILE, D_MODEL), lambda b, i: (b, i, 0)),
            pl.BlockSpec((1, N_MOD, D_MODEL), lambda b, i: (b, 0, 0)),
            _const_spec((CONV_WIDTH, CONV_DIM)),
            _const_spec((1, CONV_DIM)),
            _const_spec((1, CONV_DIM)),
            _const_spec((1, CONV_DIM)),
            _const_spec((D_MODEL, D_MODEL)),
            _const_spec((1, D_MODEL)),
        ],
        out_specs=[
            pl.BlockSpec((1, ROW_TILE, D_MODEL), lambda b, i: (b, i, 0)),
            pl.BlockSpec((1, ROW_TILE, D_MODEL), lambda b, i: (b, i, 0)),
        ],
        out_shape=[
            jax.ShapeDtypeStruct((bsz, seq, D_MODEL), F32),
            jax.ShapeDtypeStruct((bsz, seq, D_MODEL), BF16),
        ],
        scratch_shapes=[
            pltpu.VMEM((n_slab, ROW_TILE + 2 * HALO, LANES), F32),
        ],
        compiler_params=_params("parallel", "parallel"),
        name="out_proj",
    )(attn, u, u, u, x, mod, w_dw, b_dw, g_cn, b_cn, w_out, g2)


def _ffn_kernel(h2_ref, x1p_ref, mod_ref, wg_ref, wu_ref, wd_ref, gf_ref, y_ref, x1_ref):
    j = pl.program_id(2)

    @pl.when(j == 0)
    def _():
        y_ref[...] = jnp.zeros(y_ref.shape, F32)

    @pl.when(j < X1_PIECES)
    def _():
        x1_ref[pl.ds(pl.multiple_of(j * X1_PIECE_ROWS, X1_PIECE_ROWS), X1_PIECE_ROWS), :] = x1p_ref[0]

    h = h2_ref[0]
    g = _dot(h, wg_ref[...])
    u = _dot(h, wu_ref[...])
    a = (g * jax.nn.sigmoid(g) * u).astype(BF16)
    y_ref[0] += _dot(a, wd_ref[...])

    @pl.when(j == pl.num_programs(2) - 1)
    def _():
        gt2 = mod_ref[0, 5:6, :]

        def finish(r, carry):
            rows = pl.ds(pl.multiple_of(r * X1_PIECE_ROWS, X1_PIECE_ROWS), X1_PIECE_ROWS)
            x2 = x1_ref[rows, :] + gt2 * y_ref[0, rows, :]
            ms = jnp.mean(x2 * x2, axis=-1, keepdims=True)
            y_ref[0, rows, :] = x2 * lax.rsqrt(ms + EPS) * gf_ref[...]
            return carry

        lax.fori_loop(0, X1_PIECES, finish, 0)


def _ffn(h2, x1, mod, w_gate, w_up, w_down, g_final):
    bsz, seq, _ = x1.shape
    n_ff = D_FF // FF_TILE
    assert n_ff >= X1_PIECES
    return pl.pallas_call(
        _ffn_kernel,
        grid=(bsz, seq // FF_ROW_TILE, n_ff),
        in_specs=[
            pl.BlockSpec((1, FF_ROW_TILE, D_MODEL), lambda b, i, j: (b, i, 0)),
            pl.BlockSpec((1, X1_PIECE_ROWS, D_MODEL),
                         lambda b, i, j: (b, i * X1_PIECES + jnp.minimum(j, X1_PIECES - 1), 0)),
            pl.BlockSpec((1, N_MOD, D_MODEL), lambda b, i, j: (b, 0, 0)),
            pl.BlockSpec((D_MODEL, FF_TILE), lambda b, i, j: (0, j)),
            pl.BlockSpec((D_MODEL, FF_TILE), lambda b, i, j: (0, j)),
            pl.BlockSpec((FF_TILE, D_MODEL), lambda b, i, j: (j, 0)),
            _const_spec((1, D_MODEL)),
        ],
        out_specs=pl.BlockSpec((1, FF_ROW_TILE, D_MODEL), lambda b, i, j: (b, i, 0)),
        out_shape=jax.ShapeDtypeStruct((bsz, seq, D_MODEL), F32),
        scratch_shapes=[pltpu.VMEM((FF_ROW_TILE, D_MODEL), F32)],
        compiler_params=_params("parallel", "parallel", "arbitrary"),
        name="ffn",
    )(h2, x1, mod, w_gate, w_up, w_down, g_final)


def _reorder_head_dims(a):
    lead = a.shape[:-1]
    a = a.reshape(*lead, -1, 2, 2, HEAD_DIM // 4)
    return jnp.swapaxes(a, -3, -2).reshape(*lead, -1)


def _reorder_qk_columns(a):
    n_qk = (N_HEADS + N_KV_HEADS) * HEAD_DIM
    return jnp.concatenate([_reorder_head_dims(a[..., :n_qk]), a[..., n_qk:]], axis=-1)


def _layer(x, mod, tabs, p):
    cos, sin = tabs
    qt, k, vt, u, kn = _in_proj(x, mod, p["g1"], p["w_in"], p["b_in"], p["gq"], p["gk"], cos, sin)
    attn = _attention(qt, k, vt, kn)
    x1, h2 = _out_proj(attn, u, x, mod, p["w_dw"], p["b_dw"], p["g_cn"], p["b_cn"], p["w_out"], p["g2"])
    return _ffn(h2, x1, mod, p["w_gate"], p["w_up"], p["w_down"], p["g_final"])


def kernel(x_prompt, x_sample, c_prompt, c_sample, w_ada, b_ada, g_norm1, w_in, b_in, g_q, g_k, w_dw, b_dw,
           g_cn, b_cn, w_out, g_norm2, w_gate, w_up, w_down, g_final):
    assert w_ada.shape[0] == 1, "single layer"
    assert x_prompt.shape[1] == x_sample.shape[1]
    seq = x_prompt.shape[1]
    assert seq % FF_ROW_TILE == 0 and seq % Q_TILE == 0 and seq % GRID_W == 0
    nb_p, nb_s = x_prompt.shape[0], x_sample.shape[0]

    c_all = jnp.concatenate([c_prompt, c_sample], axis=0)
    pad = (-c_all.shape[0]) % SUBLANES
    c_pad = jnp.pad(c_all, ((0, pad), (0, 0)))
    mod = _modulation(c_pad, w_ada[0], b_ada[0][None, :]).reshape(c_pad.shape[0], N_MOD, D_MODEL)

    p = {
        "g1": g_norm1[0][None, :],
        "w_in": _reorder_qk_columns(w_in[0].astype(BF16)),
        "b_in": _reorder_qk_columns(b_in[0])[None, :],
        "gq": _reorder_head_dims(g_q[0])[None, :],
        "gk": _reorder_head_dims(g_k[0])[None, :],
        "w_dw": w_dw[0],
        "b_dw": b_dw[0][None, :],
        "g_cn": g_cn[0][None, :],
        "b_cn": b_cn[0][None, :],
        "w_out": w_out[0].astype(BF16),
        "g2": g_norm2[0][None, :],
        "w_gate": w_gate[0].astype(BF16),
        "w_up": w_up[0].astype(BF16),
        "w_down": w_down[0].astype(BF16),
        "g_final": g_final[None, :],
    }
    tabs = _rope_tables(seq)
    y_prompt = _layer(x_prompt, mod[:nb_p], tabs, p)
    y_sample = _layer(x_sample, mod[nb_p:nb_p + nb_s], tabs, p)
    return (y_prompt, y_sample)
```

```python
import functools
import math

import jax
import jax.numpy as jnp
from jax import lax
from jax.experimental import pallas as pl
from jax.experimental.pallas import tpu as pltpu

F32 = jnp.float32
BF16 = jnp.bfloat16

D_MODEL = 2048
HEAD_DIM = 128
N_HEADS = 8
N_KV_HEADS = 2
GROUP = N_HEADS // N_KV_HEADS
ATTN_DIM = N_HEADS * HEAD_DIM
KV_DIM = N_KV_HEADS * HEAD_DIM
CONV_DIM = D_MODEL - ATTN_DIM
CONV_WIDTH = 31
CONV_PAD = CONV_WIDTH // 2
IN_DIM = ATTN_DIM + 2 * KV_DIM + 2 * CONV_DIM
D_FF = 5632
GRID_W = 64
ROPE_THETA = 10000.0
EPS = 1e-6
N_MOD = 6

LANES = 128
SUBLANES = 8
BF16_ROWS = 16
MXU_DEPTH = 256
V7X_VMEM_LIMIT_BYTES = 56 * 1024 * 1024

ROW_TILE = 512
KV_CHUNK = 512
Q_TILE = 2048
FF_TILE = 512
FF_ROW_TILE = 1024
X1_PIECES = 8
X1_PIECE_ROWS = FF_ROW_TILE // X1_PIECES
HALO = 16
CONV_ROWS = 256
MOD_COLS = 1536

Q_SCALE = math.log2(math.e) / math.sqrt(HEAD_DIM)
SAFE_SHIFT = 48.0
NORM_MARGIN = 1.02


def _dot(a, b):
    return jnp.dot(a, b, preferred_element_type=F32)


def _params(*sem):
    return pltpu.CompilerParams(dimension_semantics=sem, vmem_limit_bytes=V7X_VMEM_LIMIT_BYTES)


def _const_spec(shape):
    nd = len(shape)
    return pl.BlockSpec(shape, lambda *_: (0,) * nd, pipeline_mode=pl.Buffered(1))


def _mod_kernel(c_ref, w_ref, b_ref, o_ref):
    c = c_ref[...]
    a = (c * jax.nn.sigmoid(c)).astype(BF16)
    o_ref[...] = _dot(a, w_ref[...].astype(BF16)) + b_ref[...]


def _modulation(c_pad, w_ada, b_ada):
    rows = c_pad.shape[0]
    n = w_ada.shape[1]
    return pl.pallas_call(
        _mod_kernel,
        grid=(n // MOD_COLS,),
        in_specs=[
            pl.BlockSpec((rows, D_MODEL), lambda j: (0, 0)),
            pl.BlockSpec((D_MODEL, MOD_COLS), lambda j: (0, j)),
            pl.BlockSpec((1, MOD_COLS), lambda j: (0, j)),
        ],
        out_specs=pl.BlockSpec((rows, MOD_COLS), lambda j: (0, j)),
        out_shape=jax.ShapeDtypeStruct((rows, n), F32),
        compiler_params=_params("arbitrary"),
        name="modulation",
    )(c_pad, w_ada, b_ada)


def _rope_kernel(cos_ref, sin_ref, tc_ref, ts_ref, *, n_rows):
    shape = tc_ref.shape
    r = lax.broadcasted_iota(jnp.int32, shape, 0)
    lane = lax.broadcasted_iota(jnp.int32, shape, 1)
    pos = jnp.where(r < n_rows, r, r - n_rows).astype(F32)
    freq = (lane % (HEAD_DIM // 4)).astype(F32)
    inv = jnp.exp(freq * (-2.0 / (HEAD_DIM // 2) * math.log(ROPE_THETA)))
    ang = pos * inv
    tc_ref[...] = jnp.cos(ang)
    ts_ref[...] = jnp.where(lane < HEAD_DIM // 2, -1.0, 1.0) * jnp.sin(ang)

    lane_b = lax.broadcasted_iota(jnp.int32, (GRID_W, LANES), 1)
    row_lane = (lane_b // (HEAD_DIM // 4)) % 2 == 0
    col_c = tc_ref[n_rows:, :]
    col_s = ts_ref[n_rows:, :]

    def body(g, carry):
        rows = pl.ds(pl.multiple_of(g * GRID_W, GRID_W), GRID_W)
        cos_ref[rows, :] = jnp.where(row_lane, tc_ref[pl.ds(g, 1), :], col_c)
        sin_ref[rows, :] = jnp.where(row_lane, ts_ref[pl.ds(g, 1), :], col_s)
        return carry

    lax.fori_loop(0, n_rows, body, 0)


def _rope_tables(seq):
    n_rows = seq // GRID_W
    table = jax.ShapeDtypeStruct((seq, LANES), F32)
    small = pltpu.VMEM((n_rows + GRID_W, LANES), F32)
    return pl.pallas_call(
        functools.partial(_rope_kernel, n_rows=n_rows),
        out_shape=(table, table),
        scratch_shapes=[small, small],
        compiler_params=pltpu.CompilerParams(vmem_limit_bytes=V7X_VMEM_LIMIT_BYTES),
        name="rope_table",
    )()


def _inproj_kernel(x_ref, mod_ref, g1_ref, w_ref, b_ref, gq_ref, gk_ref, cos_ref, sin_ref,
                   qt_ref, k_ref, vt_ref, u_ref, kn_ref):
    @pl.when(pl.program_id(1) == 0)
    def _():
        kn_ref[...] = jnp.zeros(kn_ref.shape, F32)

    x = x_ref[0]
    ms = jnp.mean(x * x, axis=-1, keepdims=True)
    sh1 = mod_ref[0, 0:1, :]
    sc1 = mod_ref[0, 1:2, :]
    h = (x * lax.rsqrt(ms + EPS) * g1_ref[...]) * (1.0 + sc1) + sh1
    hb = h.astype(BF16)
    cos = cos_ref[...]
    sin = sin_ref[...]

    def norm_rope(z, g):
        r = lax.rsqrt(jnp.mean(z * z, axis=-1, keepdims=True) + EPS)
        y = z * r * g
        return y * cos + pltpu.roll(y, HEAD_DIM // 2, 1) * sin

    k0 = ATTN_DIM
    v0 = ATTN_DIM + KV_DIM
    u0 = ATTN_DIM + 2 * KV_DIM
    zq = _dot(hb, w_ref[:, 0:k0]) + b_ref[:, 0:k0]
    zk = _dot(hb, w_ref[:, k0:v0]) + b_ref[:, k0:v0]
    zv = _dot(hb, w_ref[:, v0:u0]) + b_ref[:, v0:u0]
    za = _dot(hb, w_ref[:, u0:u0 + CONV_DIM]) + b_ref[:, u0:u0 + CONV_DIM]
    zg = _dot(hb, w_ref[:, u0 + CONV_DIM:]) + b_ref[:, u0 + CONV_DIM:]

    for hd in range(N_HEADS):
        sl = slice(hd * HEAD_DIM, (hd + 1) * HEAD_DIM)
        y = norm_rope(zq[:, sl], gq_ref[...]) * Q_SCALE
        qt_ref[0, sl, :] = y.T.astype(BF16)

    for hd in range(N_KV_HEADS):
        sl = slice(hd * HEAD_DIM, (hd + 1) * HEAD_DIM)
        kh = norm_rope(zk[:, sl], gk_ref[...])
        k_ref[0, :, sl] = kh.astype(BF16)
        n2 = jnp.max(jnp.sum(kh * kh, axis=-1, keepdims=True), axis=0, keepdims=True)
        kn_ref[0, hd:hd + 1, :] = jnp.maximum(kn_ref[0, hd:hd + 1, :], n2)

    vt_ref[0, 0] = zv.T.astype(BF16)

    glu = za * jax.nn.sigmoid(zg)
    for c in range(CONV_DIM // LANES):
        u_ref[0, c] = glu[:, c * LANES:(c + 1) * LANES]


def _in_proj(x, mod, g1, w_in, b_in, gq, gk, cos, sin):
    bsz, seq, _ = x.shape
    nt = seq // ROW_TILE
    return pl.pallas_call(
        _inproj_kernel,
        grid=(bsz, nt),
        in_specs=[
            pl.BlockSpec((1, ROW_TILE, D_MODEL), lambda b, i: (b, i, 0)),
            pl.BlockSpec((1, N_MOD, D_MODEL), lambda b, i: (b, 0, 0)),
            _const_spec((1, D_MODEL)),
            _const_spec((D_MODEL, IN_DIM)),
            _const_spec((1, IN_DIM)),
            _const_spec((1, HEAD_DIM)),
            _const_spec((1, HEAD_DIM)),
            pl.BlockSpec((ROW_TILE, LANES), lambda b, i: (i, 0)),
            pl.BlockSpec((ROW_TILE, LANES), lambda b, i: (i, 0)),
        ],
        out_specs=[
            pl.BlockSpec((1, ATTN_DIM, ROW_TILE), lambda b, i: (b, 0, i)),
            pl.BlockSpec((1, ROW_TILE, KV_DIM), lambda b, i: (b, i, 0)),
            pl.BlockSpec((1, 1, KV_DIM, KV_CHUNK), lambda b, i: (b, i, 0, 0)),
            pl.BlockSpec((1, CONV_DIM // LANES, ROW_TILE, LANES), lambda b, i: (b, 0, i, 0)),
            pl.BlockSpec((1, SUBLANES, LANES), lambda b, i: (b, 0, 0)),
        ],
        out_shape=[
            jax.ShapeDtypeStruct((bsz, ATTN_DIM, seq), BF16),
            jax.ShapeDtypeStruct((bsz, seq, KV_DIM), BF16),
            jax.ShapeDtypeStruct((bsz, nt, KV_DIM, KV_CHUNK), BF16),
            jax.ShapeDtypeStruct((bsz, CONV_DIM // LANES, seq, LANES), F32),
            jax.ShapeDtypeStruct((bsz, SUBLANES, LANES), F32),
        ],
        compiler_params=_params("parallel", "arbitrary"),
        name="in_proj",
    )(x, mod, g1, w_in, b_in, gq, gk, cos, sin)


def _attn_kernel(qt_ref, k_ref, vt_ref, kn_ref, o_ref, qa_ref, p0_ref, p1_ref, m_ref, l_ref, acc_ref, *, n_chunks):
    h = pl.program_id(1)
    tq = qt_ref.shape[2]
    gw = GROUP * tq
    k_norm = jnp.sqrt(kn_ref[0, pl.ds(h, 1), :][:, 0:1])
    row = lax.broadcasted_iota(jnp.int32, (BF16_ROWS, tq), 0)
    worst = jnp.zeros((1, 1), F32)
    for g in range(GROUP):
        q = qt_ref[0, g * HEAD_DIM:(g + 1) * HEAD_DIM, :]
        qf = q.astype(F32)
        shift = jnp.sqrt(jnp.sum(qf * qf, axis=0, keepdims=True)) * k_norm * NORM_MARGIN
        worst = jnp.maximum(worst, jnp.max(shift, axis=1, keepdims=True))
        cs = slice(g * tq, (g + 1) * tq)
        qa_ref[0:HEAD_DIM, cs] = q
        qa_ref[HEAD_DIM:HEAD_DIM + BF16_ROWS, cs] = jnp.where(row == 0, -shift, 0.0).astype(BF16)
        qa_ref[HEAD_DIM + BF16_ROWS:, cs] = jnp.zeros((MXU_DEPTH - HEAD_DIM - BF16_ROWS, tq), BF16)
    safe = worst[0, 0] <= SAFE_SHIFT

    acc_ref[...] = jnp.zeros(acc_ref.shape, F32)

    @pl.when(safe)
    def _():
        lane = lax.broadcasted_iota(jnp.int32, (KV_CHUNK, LANES), 1)
        ones_col = jnp.where(lane == 0, 1.0, 0.0).astype(BF16)

        ones_rows = jnp.ones((BF16_ROWS, KV_CHUNK), BF16)

        def qk_exp(c, p_ref):
            off = pl.multiple_of(c * KV_CHUNK, KV_CHUNK)
            ka = jnp.concatenate([k_ref[0, pl.ds(off, KV_CHUNK), :], ones_col], axis=1)
            p_ref[...] = jnp.exp2(_dot(ka, qa_ref[...])).astype(BF16)

        def pv(c, p_ref):
            va = jnp.concatenate([vt_ref[0, c], ones_rows], axis=0)
            acc_ref[...] += _dot(va, p_ref[...])

        def step(c):
            qk_exp(c + 1, p1_ref)
            pv(c, p0_ref)
            qk_exp(c + 2, p0_ref)
            pv(c + 1, p1_ref)

        def body(i, carry):
            step(4 * i)
            step(4 * i + 2)
            return carry

        trips = (n_chunks - 4) // 4
        qk_exp(0, p0_ref)
        lax.fori_loop(0, trips, body, 0)
        step(4 * trips)
        qk_exp(n_chunks - 1, p1_ref)
        pv(n_chunks - 2, p0_ref)
        pv(n_chunks - 1, p1_ref)
        first = lax.broadcasted_iota(jnp.int32, (SUBLANES, gw), 0) == 0
        l_ref[...] = jnp.where(first, acc_ref[HEAD_DIM:HEAD_DIM + 1, :], 0.0)

    @pl.when(jnp.logical_not(safe))
    def _():
        m_ref[...] = jnp.full(m_ref.shape, -jnp.inf, F32)
        l_ref[...] = jnp.zeros(l_ref.shape, F32)
        first_row = lax.broadcasted_iota(jnp.int32, (SUBLANES, tq), 0) == 0

        def body(c, carry):
            off = pl.multiple_of(c * KV_CHUNK, KV_CHUNK)
            kb = k_ref[0, pl.ds(off, KV_CHUNK), :]
            vb = vt_ref[0, c]
            for g in range(GROUP):
                cs = slice(g * tq, (g + 1) * tq)
                s = _dot(kb, qt_ref[0, g * HEAD_DIM:(g + 1) * HEAD_DIM, :])
                m_old = m_ref[:, cs]
                m_new = jnp.maximum(m_old, jnp.max(s, axis=0, keepdims=True))
                alpha = jnp.exp2(m_old - m_new)
                p = jnp.exp2(s - m_new)
                l_ref[:, cs] = alpha * l_ref[:, cs] + jnp.where(first_row, jnp.sum(p, axis=0, keepdims=True), 0.0)
                acc_ref[0:HEAD_DIM, cs] = alpha * acc_ref[0:HEAD_DIM, cs] + _dot(vb, p.astype(BF16))
                m_ref[:, cs] = m_new
            return carry

        lax.fori_loop(0, n_chunks, body, 0)

    o = acc_ref[0:HEAD_DIM, :] * (1.0 / jnp.sum(l_ref[...], axis=0, keepdims=True))
    for g in range(GROUP):
        o_ref[0, :, g * HEAD_DIM:(g + 1) * HEAD_DIM] = o[:, g * tq:(g + 1) * tq].T.astype(BF16)


def _attention(qt, k, vt, kn):
    bsz, _, seq = qt.shape
    n_chunks = seq // KV_CHUNK
    assert n_chunks % 4 == 0
    gw = GROUP * HEAD_DIM
    return pl.pallas_call(
        functools.partial(_attn_kernel, n_chunks=n_chunks),
        grid=(bsz, N_KV_HEADS, seq // Q_TILE),
        in_specs=[
            pl.BlockSpec((1, gw, Q_TILE), lambda b, h, i: (b, h, i)),
            pl.BlockSpec((1, seq, HEAD_DIM), lambda b, h, i: (b, 0, h)),
            pl.BlockSpec((1, n_chunks, HEAD_DIM, KV_CHUNK), lambda b, h, i: (b, 0, h, 0)),
            pl.BlockSpec((1, SUBLANES, LANES), lambda b, h, i: (b, 0, 0)),
        ],
        out_specs=pl.BlockSpec((1, Q_TILE, gw), lambda b, h, i: (b, i, h)),
        out_shape=jax.ShapeDtypeStruct((bsz, seq, ATTN_DIM), BF16),
        scratch_shapes=[
            pltpu.VMEM((MXU_DEPTH, GROUP * Q_TILE), BF16),
            pltpu.VMEM((KV_CHUNK, GROUP * Q_TILE), BF16),
            pltpu.VMEM((KV_CHUNK, GROUP * Q_TILE), BF16),
            pltpu.VMEM((1, GROUP * Q_TILE), F32),
            pltpu.VMEM((SUBLANES, GROUP * Q_TILE), F32),
            pltpu.VMEM((HEAD_DIM + BF16_ROWS, GROUP * Q_TILE), F32),
        ],
        compiler_params=_params("parallel", "parallel", "arbitrary"),
        name="attention",
    )(qt, k, vt, kn)


def _outproj_kernel(attn_ref, u_ref, up_ref, un_ref, x_ref, mod_ref, wdw_ref, bdw_ref, gcn_ref, bcn_ref,
                    wout_ref, g2_ref, x1_ref, h2_ref, buf_ref):
    i = pl.program_id(1)
    last = pl.num_programs(1) - 1
    buf_ref[:, 0:HALO, :] = jnp.where(i > 0, up_ref[0], 0.0)
    buf_ref[:, HALO:HALO + ROW_TILE, :] = u_ref[0]
    buf_ref[:, HALO + ROW_TILE:, :] = jnp.where(i < last, un_ref[0], 0.0)

    gt1 = mod_ref[0, 2:3, :]
    sh2 = mod_ref[0, 3:4, :]
    g2s = g2_ref[...] * (1.0 + mod_ref[0, 4:5, :])
    base = HALO - CONV_PAD
    for r0 in range(0, ROW_TILE, CONV_ROWS):
        rows = slice(r0, r0 + CONV_ROWS)
        cols = []
        for c in range(CONV_DIM // LANES):
            cs = slice(c * LANES, (c + 1) * LANES)
            acc = jnp.broadcast_to(bdw_ref[:, cs], (CONV_ROWS, LANES))
            for j in range(CONV_WIDTH):
                lo = base + r0 + j
                acc = acc + buf_ref[c, lo:lo + CONV_ROWS, :] * wdw_ref[j:j + 1, cs]
            cols.append(acc)
        hc = jnp.concatenate(cols, axis=1)
        mu = jnp.mean(hc, axis=-1, keepdims=True)
        d = hc - mu
        var = jnp.mean(d * d, axis=-1, keepdims=True)
        y = d * lax.rsqrt(var + EPS) * gcn_ref[...] + bcn_ref[...]
        conv = (y * jax.nn.sigmoid(y)).astype(BF16)

        mix = _dot(jnp.concatenate([attn_ref[0, rows, :], conv], axis=1), wout_ref[...])
        x1 = x_ref[0, rows, :] + gt1 * mix
        x1_ref[0, rows, :] = x1
        ms = jnp.mean(x1 * x1, axis=-1, keepdims=True)
        h2_ref[0, rows, :] = (x1 * lax.rsqrt(ms + EPS) * g2s + sh2).astype(BF16)


def _out_proj(attn, u, x, mod, w_dw, b_dw, g_cn, b_cn, w_out, g2):
    bsz, seq, _ = x.shape
    nt = seq // ROW_TILE
    hb = ROW_TILE // HALO
    n_halo = seq // HALO
    n_slab = CONV_DIM // LANES
    return pl.pallas_call(
        _outproj_kernel,
        grid=(bsz, nt),
        in_specs=[
            pl.BlockSpec((1, ROW_TILE, ATTN_DIM), lambda b, i: (b, i, 0)),
            pl.BlockSpec((1, n_slab, ROW_TILE, LANES), lambda b, i: (b, 0, i, 0)),
            pl.BlockSpec((1, n_slab, HALO, LANES), lambda b, i: (b, 0, jnp.maximum(i * hb - 1, 0), 0)),
            pl.BlockSpec((1, n_slab, HALO, LANES), lambda b, i: (b, 0, jnp.minimum((i + 1) * hb, n_halo - 1), 0)),
            pl.BlockSpec((1, ROW_TILE, D_MODEL), lambda b, i: (b, i, 0)),
            pl.BlockSpec((1, N_MOD, D_MODEL), lambda b, i: (b, 0, 0)),
            _const_spec((CONV_WIDTH, CONV_DIM)),
            _const_spec((1, CONV_DIM)),
            _const_spec((1, CONV_DIM)),
            _const_spec((1, CONV_DIM)),
            _const_spec((D_MODEL, D_MODEL)),
            _const_spec((1, D_MODEL)),
        ],
        out_specs=[
            pl.BlockSpec((1, ROW_TILE, D_MODEL), lambda b, i: (b, i, 0)),
            pl.BlockSpec((1, ROW_TILE, D_MODEL), lambda b, i: (b, i, 0)),
        ],
        out_shape=[
            jax.ShapeDtypeStruct((bsz, seq, D_MODEL), F32),
            jax.ShapeDtypeStruct((bsz, seq, D_MODEL), BF16),
        ],
        scratch_shapes=[
            pltpu.VMEM((n_slab, ROW_TILE + 2 * HALO, LANES), F32),
        ],
        compiler_params=_params("parallel", "parallel"),
        name="out_proj",
    )(attn, u, u, u, x, mod, w_dw, b_dw, g_cn, b_cn, w_out, g2)


def _ffn_kernel(h2_ref, x1p_ref, mod_ref, wg_ref, wu_ref, wd_ref, gf_ref, y_ref, x1_ref):
    j = pl.program_id(2)

    @pl.when(j == 0)
    def _():
        y_ref[...] = jnp.zeros(y_ref.shape, F32)

    @pl.when(j < X1_PIECES)
    def _():
        x1_ref[pl.ds(pl.multiple_of(j * X1_PIECE_ROWS, X1_PIECE_ROWS), X1_PIECE_ROWS), :] = x1p_ref[0]

    h = h2_ref[0]
    g = _dot(h, wg_ref[...])
    u = _dot(h, wu_ref[...])
    a = (g * jax.nn.sigmoid(g) * u).astype(BF16)
    y_ref[0] += _dot(a, wd_ref[...])

    @pl.when(j == pl.num_programs(2) - 1)
    def _():
        gt2 = mod_ref[0, 5:6, :]

        def finish(r, carry):
            rows = pl.ds(pl.multiple_of(r * X1_PIECE_ROWS, X1_PIECE_ROWS), X1_PIECE_ROWS)
            x2 = x1_ref[rows, :] + gt2 * y_ref[0, rows, :]
            ms = jnp.mean(x2 * x2, axis=-1, keepdims=True)
            y_ref[0, rows, :] = x2 * lax.rsqrt(ms + EPS) * gf_ref[...]
            return carry

        lax.fori_loop(0, X1_PIECES, finish, 0)


def _ffn(h2, x1, mod, w_gate, w_up, w_down, g_final):
    bsz, seq, _ = x1.shape
    n_ff = D_FF // FF_TILE
    assert n_ff >= X1_PIECES
    return pl.pallas_call(
        _ffn_kernel,
        grid=(bsz, seq // FF_ROW_TILE, n_ff),
        in_specs=[
            pl.BlockSpec((1, FF_ROW_TILE, D_MODEL), lambda b, i, j: (b, i, 0)),
            pl.BlockSpec((1, X1_PIECE_ROWS, D_MODEL),
                         lambda b, i, j: (b, i * X1_PIECES + jnp.minimum(j, X1_PIECES - 1), 0)),
            pl.BlockSpec((1, N_MOD, D_MODEL), lambda b, i, j: (b, 0, 0)),
            pl.BlockSpec((D_MODEL, FF_TILE), lambda b, i, j: (0, j)),
            pl.BlockSpec((D_MODEL, FF_TILE), lambda b, i, j: (0, j)),
            pl.BlockSpec((FF_TILE, D_MODEL), lambda b, i, j: (j, 0)),
            _const_spec((1, D_MODEL)),
        ],
        out_specs=pl.BlockSpec((1, FF_ROW_TILE, D_MODEL), lambda b, i, j: (b, i, 0)),
        out_shape=jax.ShapeDtypeStruct((bsz, seq, D_MODEL), F32),
        scratch_shapes=[pltpu.VMEM((FF_ROW_TILE, D_MODEL), F32)],
        compiler_params=_params("parallel", "parallel", "arbitrary"),
        name="ffn",
    )(h2, x1, mod, w_gate, w_up, w_down, g_final)


def _reorder_head_dims(a):
    lead = a.shape[:-1]
    a = a.reshape(*lead, -1, 2, 2, HEAD_DIM // 4)
    return jnp.swapaxes(a, -3, -2).reshape(*lead, -1)


def _reorder_qk_columns(a):
    n_qk = (N_HEADS + N_KV_HEADS) * HEAD_DIM
    return jnp.concatenate([_reorder_head_dims(a[..., :n_qk]), a[..., n_qk:]], axis=-1)


def _layer(x, mod, tabs, p):
    cos, sin = tabs
    qt, k, vt, u, kn = _in_proj(x, mod, p["g1"], p["w_in"], p["b_in"], p["gq"], p["gk"], cos, sin)
    attn = _attention(qt, k, vt, kn)
    x1, h2 = _out_proj(attn, u, x, mod, p["w_dw"], p["b_dw"], p["g_cn"], p["b_cn"], p["w_out"], p["g2"])
    return _ffn(h2, x1, mod, p["w_gate"], p["w_up"], p["w_down"], p["g_final"])


def kernel(x_prompt, x_sample, c_prompt, c_sample, w_ada, b_ada, g_norm1, w_in, b_in, g_q, g_k, w_dw, b_dw,
           g_cn, b_cn, w_out, g_norm2, w_gate, w_up, w_down, g_final):
    assert w_ada.shape[0] == 1, "single layer"
    assert x_prompt.shape[1] == x_sample.shape[1]
    seq = x_prompt.shape[1]
    assert seq % FF_ROW_TILE == 0 and seq % Q_TILE == 0 and seq % GRID_W == 0
    nb_p, nb_s = x_prompt.shape[0], x_sample.shape[0]

    c_all = jnp.concatenate([c_prompt, c_sample], axis=0)
    pad = (-c_all.shape[0]) % SUBLANES
    c_pad = jnp.pad(c_all, ((0, pad), (0, 0)))
    mod = _modulation(c_pad, w_ada[0], b_ada[0][None, :]).reshape(c_pad.shape[0], N_MOD, D_MODEL)

    p = {
        "g1": g_norm1[0][None, :],
        "w_in": _reorder_qk_columns(w_in[0].astype(BF16)),
        "b_in": _reorder_qk_columns(b_in[0])[None, :],
        "gq": _reorder_head_dims(g_q[0])[None, :],
        "gk": _reorder_head_dims(g_k[0])[None, :],
        "w_dw": w_dw[0],
        "b_dw": b_dw[0][None, :],
        "g_cn": g_cn[0][None, :],
        "b_cn": b_cn[0][None, :],
        "w_out": w_out[0].astype(BF16),
        "g2": g_norm2[0][None, :],
        "w_gate": w_gate[0].astype(BF16),
        "w_up": w_up[0].astype(BF16),
        "w_down": w_down[0].astype(BF16),
        "g_final": g_final[None, :],
    }
    tabs = _rope_tables(seq)
    y_prompt = _layer(x_prompt, mod[:nb_p], tabs, p)
    y_sample = _layer(x_sample, mod[nb_p:nb_p + nb_s], tabs, p)
    return (y_prompt, y_sample)
```

```python
import functools
import math

import jax
import jax.numpy as jnp
from jax import lax
from jax.experimental import pallas as pl
from jax.experimental.pallas import tpu as pltpu

F32 = jnp.float32
BF16 = jnp.bfloat16

D_MODEL = 2048
HEAD_DIM = 128
N_HEADS = 8
N_KV_HEADS = 2
GROUP = N_HEADS // N_KV_HEADS
ATTN_DIM = N_HEADS * HEAD_DIM
KV_DIM = N_KV_HEADS * HEAD_DIM
CONV_DIM = D_MODEL - ATTN_DIM
CONV_WIDTH = 31
CONV_PAD = CONV_WIDTH // 2
IN_DIM = ATTN_DIM + 2 * KV_DIM + 2 * CONV_DIM
D_FF = 5632
GRID_W = 64
ROPE_THETA = 10000.0
EPS = 1e-6
N_MOD = 6

LANES = 128
SUBLANES = 8
BF16_ROWS = 16
MXU_DEPTH = 256
V7X_VMEM_LIMIT_BYTES = 56 * 1024 * 1024

ROW_TILE = 512
KV_CHUNK = 512
Q_TILE = 2048
FF_TILE = 512
FF_ROW_TILE = 1024
X1_PIECES = 8
X1_PIECE_ROWS = FF_ROW_TILE // X1_PIECES
HALO = 16
CONV_ROWS = 256
MOD_COLS = 1536

Q_SCALE = math.log2(math.e) / math.sqrt(HEAD_DIM)
SAFE_SHIFT = 48.0
NORM_MARGIN = 1.02


def _dot(a, b):
    return jnp.dot(a, b, preferred_element_type=F32)


def _params(*sem):
    return pltpu.CompilerParams(dimension_semantics=sem, vmem_limit_bytes=V7X_VMEM_LIMIT_BYTES)


def _const_spec(shape):
    nd = len(shape)
    return pl.BlockSpec(shape, lambda *_: (0,) * nd, pipeline_mode=pl.Buffered(1))


def _mod_kernel(c_ref, w_ref, b_ref, o_ref):
    c = c_ref[...]
    a = (c * jax.nn.sigmoid(c)).astype(BF16)
    o_ref[...] = _dot(a, w_ref[...].astype(BF16)) + b_ref[...]


def _modulation(c_pad, w_ada, b_ada):
    rows = c_pad.shape[0]
    n = w_ada.shape[1]
    return pl.pallas_call(
        _mod_kernel,
        grid=(n // MOD_COLS,),
        in_specs=[
            pl.BlockSpec((rows, D_MODEL), lambda j: (0, 0)),
            pl.BlockSpec((D_MODEL, MOD_COLS), lambda j: (0, j)),
            pl.BlockSpec((1, MOD_COLS), lambda j: (0, j)),
        ],
        out_specs=pl.BlockSpec((rows, MOD_COLS), lambda j: (0, j)),
        out_shape=jax.ShapeDtypeStruct((rows, n), F32),
        compiler_params=_params("arbitrary"),
        name="modulation",
    )(c_pad, w_ada, b_ada)


def _rope_kernel(cos_ref, sin_ref, tc_ref, ts_ref, *, n_rows):
    shape = tc_ref.shape
    r = lax.broadcasted_iota(jnp.int32, shape, 0)
    lane = lax.broadcasted_iota(jnp.int32, shape, 1)
    pos = jnp.where(r < n_rows, r, r - n_rows).astype(F32)
    freq = (lane % (HEAD_DIM // 4)).astype(F32)
    inv = jnp.exp(freq * (-2.0 / (HEAD_DIM // 2) * math.log(ROPE_THETA)))
    ang = pos * inv
    tc_ref[...] = jnp.cos(ang)
    ts_ref[...] = jnp.where(lane < HEAD_DIM // 2, -1.0, 1.0) * jnp.sin(ang)

    lane_b = lax.broadcasted_iota(jnp.int32, (GRID_W, LANES), 1)
    row_lane = (lane_b // (HEAD_DIM // 4)) % 2 == 0
    col_c = tc_ref[n_rows:, :]
    col_s = ts_ref[n_rows:, :]

    def body(g, carry):
        rows = pl.ds(pl.multiple_of(g * GRID_W, GRID_W), GRID_W)
        cos_ref[rows, :] = jnp.where(row_lane, tc_ref[pl.ds(g, 1), :], col_c)
        sin_ref[rows, :] = jnp.where(row_lane, ts_ref[pl.ds(g, 1), :], col_s)
        return carry

    lax.fori_loop(0, n_rows, body, 0)


def _rope_tables(seq):
    n_rows = seq // GRID_W
    table = jax.ShapeDtypeStruct((seq, LANES), F32)
    small = pltpu.VMEM((n_rows + GRID_W, LANES), F32)
    return pl.pallas_call(
        functools.partial(_rope_kernel, n_rows=n_rows),
        out_shape=(table, table),
        scratch_shapes=[small, small],
        compiler_params=pltpu.CompilerParams(vmem_limit_bytes=V7X_VMEM_LIMIT_BYTES),
        name="rope_table",
    )()


def _inproj_kernel(x_ref, mod_ref, g1_ref, w_ref, b_ref, gq_ref, gk_ref, cos_ref, sin_ref,
                   qt_ref, k_ref, vt_ref, u_ref, kn_ref):
    @pl.when(pl.program_id(1) == 0)
    def _():
        kn_ref[...] = jnp.zeros(kn_ref.shape, F32)

    x = x_ref[0]
    ms = jnp.mean(x * x, axis=-1, keepdims=True)
    sh1 = mod_ref[0, 0:1, :]
    sc1 = mod_ref[0, 1:2, :]
    h = (x * lax.rsqrt(ms + EPS) * g1_ref[...]) * (1.0 + sc1) + sh1
    hb = h.astype(BF16)
    cos = cos_ref[...]
    sin = sin_ref[...]

    def norm_rope(z, g):
        r = lax.rsqrt(jnp.mean(z * z, axis=-1, keepdims=True) + EPS)
        y = z * r * g
        return y * cos + pltpu.roll(y, HEAD_DIM // 2, 1) * sin

    k0 = ATTN_DIM
    v0 = ATTN_DIM + KV_DIM
    u0 = ATTN_DIM + 2 * KV_DIM
    zq = _dot(hb, w_ref[:, 0:k0]) + b_ref[:, 0:k0]
    zk = _dot(hb, w_ref[:, k0:v0]) + b_ref[:, k0:v0]
    zv = _dot(hb, w_ref[:, v0:u0]) + b_ref[:, v0:u0]
    za = _dot(hb, w_ref[:, u0:u0 + CONV_DIM]) + b_ref[:, u0:u0 + CONV_DIM]
    zg = _dot(hb, w_ref[:, u0 + CONV_DIM:]) + b_ref[:, u0 + CONV_DIM:]

    for hd in range(N_HEADS):
        sl = slice(hd * HEAD_DIM, (hd + 1) * HEAD_DIM)
        y = norm_rope(zq[:, sl], gq_ref[...]) * Q_SCALE
        qt_ref[0, sl, :] = y.T.astype(BF16)

    for hd in range(N_KV_HEADS):
        sl = slice(hd * HEAD_DIM, (hd + 1) * HEAD_DIM)
        kh = norm_rope(zk[:, sl], gk_ref[...])
        k_ref[0, :, sl] = kh.astype(BF16)
        n2 = jnp.max(jnp.sum(kh * kh, axis=-1, keepdims=True), axis=0, keepdims=True)
        kn_ref[0, hd:hd + 1, :] = jnp.maximum(kn_ref[0, hd:hd + 1, :], n2)

    vt_ref[0, 0] = zv.T.astype(BF16)

    glu = za * jax.nn.sigmoid(zg)
    for c in range(CONV_DIM // LANES):
        u_ref[0, c] = glu[:, c * LANES:(c + 1) * LANES]


def _in_proj(x, mod, g1, w_in, b_in, gq, gk, cos, sin):
    bsz, seq, _ = x.shape
    nt = seq // ROW_TILE
    return pl.pallas_call(
        _inproj_kernel,
        grid=(bsz, nt),
        in_specs=[
            pl.BlockSpec((1, ROW_TILE, D_MODEL), lambda b, i: (b, i, 0)),
            pl.BlockSpec((1, N_MOD, D_MODEL), lambda b, i: (b, 0, 0)),
            _const_spec((1, D_MODEL)),
            _const_spec((D_MODEL, IN_DIM)),
            _const_spec((1, IN_DIM)),
            _const_spec((1, HEAD_DIM)),
            _const_spec((1, HEAD_DIM)),
            pl.BlockSpec((ROW_TILE, LANES), lambda b, i: (i, 0)),
            pl.BlockSpec((ROW_TILE, LANES), lambda b, i: (i, 0)),
        ],
        out_specs=[
            pl.BlockSpec((1, ATTN_DIM, ROW_TILE), lambda b, i: (b, 0, i)),
            pl.BlockSpec((1, ROW_TILE, KV_DIM), lambda b, i: (b, i, 0)),
            pl.BlockSpec((1, 1, KV_DIM, KV_CHUNK), lambda b, i: (b, i, 0, 0)),
            pl.BlockSpec((1, CONV_DIM // LANES, ROW_TILE, LANES), lambda b, i: (b, 0, i, 0)),
            pl.BlockSpec((1, SUBLANES, LANES), lambda b, i: (b, 0, 0)),
        ],
        out_shape=[
            jax.ShapeDtypeStruct((bsz, ATTN_DIM, seq), BF16),
            jax.ShapeDtypeStruct((bsz, seq, KV_DIM), BF16),
            jax.ShapeDtypeStruct((bsz, nt, KV_DIM, KV_CHUNK), BF16),
            jax.ShapeDtypeStruct((bsz, CONV_DIM // LANES, seq, LANES), F32),
            jax.ShapeDtypeStruct((bsz, SUBLANES, LANES), F32),
        ],
        compiler_params=_params("parallel", "arbitrary"),
        name="in_proj",
    )(x, mod, g1, w_in, b_in, gq, gk, cos, sin)


def _attn_kernel(qt_ref, k_ref, vt_ref, kn_ref, o_ref, qa_ref, p0_ref, p1_ref, m_ref, l_ref, acc_ref, *, n_chunks):
    h = pl.program_id(1)
    tq = qt_ref.shape[2]
    gw = GROUP * tq
    k_norm = jnp.sqrt(kn_ref[0, pl.ds(h, 1), :][:, 0:1])
    row = lax.broadcasted_iota(jnp.int32, (BF16_ROWS, tq), 0)
    worst = jnp.zeros((1, 1), F32)
    for g in range(GROUP):
        q = qt_ref[0, g * HEAD_DIM:(g + 1) * HEAD_DIM, :]
        qf = q.astype(F32)
        shift = jnp.sqrt(jnp.sum(qf * qf, axis=0, keepdims=True)) * k_norm * NORM_MARGIN
        worst = jnp.maximum(worst, jnp.max(shift, axis=1, keepdims=True))
        cs = slice(g * tq, (g + 1) * tq)
        qa_ref[0:HEAD_DIM, cs] = q
        qa_ref[HEAD_DIM:HEAD_DIM + BF16_ROWS, cs] = jnp.where(row == 0, -shift, 0.0).astype(BF16)
        qa_ref[HEAD_DIM + BF16_ROWS:, cs] = jnp.zeros((MXU_DEPTH - HEAD_DIM - BF16_ROWS, tq), BF16)
    safe = worst[0, 0] <= SAFE_SHIFT

    acc_ref[...] = jnp.zeros(acc_ref.shape, F32)

    @pl.when(safe)
    def _():
        lane = lax.broadcasted_iota(jnp.int32, (KV_CHUNK, LANES), 1)
        ones_col = jnp.where(lane == 0, 1.0, 0.0).astype(BF16)

        def qk_exp(c, p_ref):
            off = pl.multiple_of(c * KV_CHUNK, KV_CHUNK)
            ka = jnp.concatenate([k_ref[0, pl.ds(off, KV_CHUNK), :], ones_col], axis=1)
            p = jnp.exp2(_dot(ka, qa_ref[...]))
            p_ref[...] = p.astype(BF16)
            return jnp.sum(p.reshape(KV_CHUNK // SUBLANES, SUBLANES, gw), axis=0)

        def pv(c, p_ref):
            acc_ref[...] += _dot(vt_ref[0, c], p_ref[...])

        def step(c, l):
            l = l + qk_exp(c + 1, p1_ref)
            pv(c, p0_ref)
            l = l + qk_exp(c + 2, p0_ref)
            pv(c + 1, p1_ref)
            return l

        def body(i, l):
            return step(4 * i + 2, step(4 * i, l))

        trips = (n_chunks - 4) // 4
        l = lax.fori_loop(0, trips, body, qk_exp(0, p0_ref))
        l = step(4 * trips, l)
        l = l + qk_exp(n_chunks - 1, p1_ref)
        pv(n_chunks - 2, p0_ref)
        pv(n_chunks - 1, p1_ref)
        l_ref[...] = l

    @pl.when(jnp.logical_not(safe))
    def _():
        m_ref[...] = jnp.full(m_ref.shape, -jnp.inf, F32)
        l_ref[...] = jnp.zeros(l_ref.shape, F32)
        first_row = lax.broadcasted_iota(jnp.int32, (SUBLANES, tq), 0) == 0

        def body(c, carry):
            off = pl.multiple_of(c * KV_CHUNK, KV_CHUNK)
            kb = k_ref[0, pl.ds(off, KV_CHUNK), :]
            vb = vt_ref[0, c]
            for g in range(GROUP):
                cs = slice(g * tq, (g + 1) * tq)
                s = _dot(kb, qt_ref[0, g * HEAD_DIM:(g + 1) * HEAD_DIM, :])
                m_old = m_ref[:, cs]
                m_new = jnp.maximum(m_old, jnp.max(s, axis=0, keepdims=True))
                alpha = jnp.exp2(m_old - m_new)
                p = jnp.exp2(s - m_new)
                l_ref[:, cs] = alpha * l_ref[:, cs] + jnp.where(first_row, jnp.sum(p, axis=0, keepdims=True), 0.0)
                acc_ref[:, cs] = alpha * acc_ref[:, cs] + _dot(vb, p.astype(BF16))
                m_ref[:, cs] = m_new
            return carry

        lax.fori_loop(0, n_chunks, body, 0)

    o = acc_ref[...] * (1.0 / jnp.sum(l_ref[...], axis=0, keepdims=True))
    for g in range(GROUP):
        o_ref[0, :, g * HEAD_DIM:(g + 1) * HEAD_DIM] = o[:, g * tq:(g + 1) * tq].T.astype(BF16)


def _attention(qt, k, vt, kn):
    bsz, _, seq = qt.shape
    n_chunks = seq // KV_CHUNK
    assert n_chunks % 4 == 0
    gw = GROUP * HEAD_DIM
    return pl.pallas_call(
        functools.partial(_attn_kernel, n_chunks=n_chunks),
        grid=(bsz, N_KV_HEADS, seq // Q_TILE),
        in_specs=[
            pl.BlockSpec((1, gw, Q_TILE), lambda b, h, i: (b, h, i)),
            pl.BlockSpec((1, seq, HEAD_DIM), lambda b, h, i: (b, 0, h)),
            pl.BlockSpec((1, n_chunks, HEAD_DIM, KV_CHUNK), lambda b, h, i: (b, 0, h, 0)),
            pl.BlockSpec((1, SUBLANES, LANES), lambda b, h, i: (b, 0, 0)),
        ],
        out_specs=pl.BlockSpec((1, Q_TILE, gw), lambda b, h, i: (b, i, h)),
        out_shape=jax.ShapeDtypeStruct((bsz, seq, ATTN_DIM), BF16),
        scratch_shapes=[
            pltpu.VMEM((MXU_DEPTH, GROUP * Q_TILE), BF16),
            pltpu.VMEM((KV_CHUNK, GROUP * Q_TILE), BF16),
            pltpu.VMEM((KV_CHUNK, GROUP * Q_TILE), BF16),
            pltpu.VMEM((1, GROUP * Q_TILE), F32),
            pltpu.VMEM((SUBLANES, GROUP * Q_TILE), F32),
            pltpu.VMEM((HEAD_DIM, GROUP * Q_TILE), F32),
        ],
        compiler_params=_params("parallel", "parallel", "arbitrary"),
        name="attention",
    )(qt, k, vt, kn)


def _outproj_kernel(attn_ref, u_ref, up_ref, un_ref, x_ref, mod_ref, wdw_ref, bdw_ref, gcn_ref, bcn_ref,
                    wout_ref, g2_ref, x1_ref, h2_ref, buf_ref):
    i = pl.program_id(1)
    last = pl.num_programs(1) - 1
    buf_ref[:, 0:HALO, :] = jnp.where(i > 0, up_ref[0], 0.0)
    buf_ref[:, HALO:HALO + ROW_TILE, :] = u_ref[0]
    buf_ref[:, HALO + ROW_TILE:, :] = jnp.where(i < last, un_ref[0], 0.0)

    gt1 = mod_ref[0, 2:3, :]
    sh2 = mod_ref[0, 3:4, :]
    g2s = g2_ref[...] * (1.0 + mod_ref[0, 4:5, :])
    base = HALO - CONV_PAD
    for r0 in range(0, ROW_TILE, CONV_ROWS):
        rows = slice(r0, r0 + CONV_ROWS)
        cols = []
        for c in range(CONV_DIM // LANES):
            cs = slice(c * LANES, (c + 1) * LANES)
            acc = jnp.broadcast_to(bdw_ref[:, cs], (CONV_ROWS, LANES))
            for j in range(CONV_WIDTH):
                lo = base + r0 + j
                acc = acc + buf_ref[c, lo:lo + CONV_ROWS, :] * wdw_ref[j:j + 1, cs]
            cols.append(acc)
        hc = jnp.concatenate(cols, axis=1)
        mu = jnp.mean(hc, axis=-1, keepdims=True)
        d = hc - mu
        var = jnp.mean(d * d, axis=-1, keepdims=True)
        y = d * lax.rsqrt(var + EPS) * gcn_ref[...] + bcn_ref[...]
        conv = (y * jax.nn.sigmoid(y)).astype(BF16)

        mix = _dot(jnp.concatenate([attn_ref[0, rows, :], conv], axis=1), wout_ref[...])
        x1 = x_ref[0, rows, :] + gt1 * mix
        x1_ref[0, rows, :] = x1
        ms = jnp.mean(x1 * x1, axis=-1, keepdims=True)
        h2_ref[0, rows, :] = (x1 * lax.rsqrt(ms + EPS) * g2s + sh2).astype(BF16)


def _out_proj(attn, u, x, mod, w_dw, b_dw, g_cn, b_cn, w_out, g2):
    bsz, seq, _ = x.shape
    nt = seq // ROW_TILE
    hb = ROW_TILE // HALO
    n_halo = seq // HALO
    n_slab = CONV_DIM // LANES
    return pl.pallas_call(
        _outproj_kernel,
        grid=(bsz, nt),
        in_specs=[
            pl.BlockSpec((1, ROW_TILE, ATTN_DIM), lambda b, i: (b, i, 0)),
            pl.BlockSpec((1, n_slab, ROW_TILE, LANES), lambda b, i: (b, 0, i, 0)),
            pl.BlockSpec((1, n_slab, HALO, LANES), lambda b, i: (b, 0, jnp.maximum(i * hb - 1, 0), 0)),
            pl.BlockSpec((1, n_slab, HALO, LANES), lambda b, i: (b, 0, jnp.minimum((i + 1) * hb, n_halo - 1), 0)),
            pl.BlockSpec((1, ROW_TILE, D_MODEL), lambda b, i: (b, i, 0)),
            pl.BlockSpec((1, N_MOD, D_MODEL), lambda b, i: (b, 0, 0)),
            _const_spec((CONV_WIDTH, CONV_DIM)),
            _const_spec((1, CONV_DIM)),
            _const_spec((1, CONV_DIM)),
            _const_spec((1, CONV_DIM)),
            _const_spec((D_MODEL, D_MODEL)),
            _const_spec((1, D_MODEL)),
        ],
        out_specs=[
            pl.BlockSpec((1, ROW_TILE, D_MODEL), lambda b, i: (b, i, 0)),
            pl.BlockSpec((1, ROW_TILE, D_MODEL), lambda b, i: (b, i, 0)),
        ],
        out_shape=[
            jax.ShapeDtypeStruct((bsz, seq, D_MODEL), F32),
            jax.ShapeDtypeStruct((bsz, seq, D_MODEL), BF16),
        ],
        scratch_shapes=[
            pltpu.VMEM((n_slab, ROW_TILE + 2 * HALO, LANES), F32),
        ],
        compiler_params=_params("parallel", "parallel"),
        name="out_proj",
    )(attn, u, u, u, x, mod, w_dw, b_dw, g_cn, b_cn, w_out, g2)


def _ffn_kernel(h2_ref, x1p_ref, mod_ref, wg_ref, wu_ref, wd_ref, gf_ref, y_ref, x1_ref):
    j = pl.program_id(2)

    piece = jnp.minimum(j, X1_PIECES - 1)
    x1_ref[pl.ds(pl.multiple_of(piece * X1_PIECE_ROWS, X1_PIECE_ROWS), X1_PIECE_ROWS), :] = x1p_ref[0]

    h = h2_ref[0]
    g = _dot(h, wg_ref[...])
    u = _dot(h, wu_ref[...])
    a = (g * jax.nn.sigmoid(g) * u).astype(BF16)
    y_ref[0] = jnp.where(j == 0, 0.0, y_ref[0]) + _dot(a, wd_ref[...])

    @pl.when(j == pl.num_programs(2) - 1)
    def _():
        gt2 = mod_ref[0, 5:6, :]

        def finish(r, carry):
            rows = pl.ds(pl.multiple_of(r * X1_PIECE_ROWS, X1_PIECE_ROWS), X1_PIECE_ROWS)
            x2 = x1_ref[rows, :] + gt2 * y_ref[0, rows, :]
            ms = jnp.mean(x2 * x2, axis=-1, keepdims=True)
            y_ref[0, rows, :] = x2 * lax.rsqrt(ms + EPS) * gf_ref[...]
            return carry

        lax.fori_loop(0, X1_PIECES, finish, 0)


def _ffn(h2, x1, mod, w_gate, w_up, w_down, g_final):
    bsz, seq, _ = x1.shape
    n_ff = D_FF // FF_TILE
    assert n_ff >= X1_PIECES
    return pl.pallas_call(
        _ffn_kernel,
        grid=(bsz, seq // FF_ROW_TILE, n_ff),
        in_specs=[
            pl.BlockSpec((1, FF_ROW_TILE, D_MODEL), lambda b, i, j: (b, i, 0)),
            pl.BlockSpec((1, X1_PIECE_ROWS, D_MODEL),
                         lambda b, i, j: (b, i * X1_PIECES + jnp.minimum(j, X1_PIECES - 1), 0)),
            pl.BlockSpec((1, N_MOD, D_MODEL), lambda b, i, j: (b, 0, 0)),
            pl.BlockSpec((D_MODEL, FF_TILE), lambda b, i, j: (0, j)),
            pl.BlockSpec((D_MODEL, FF_TILE), lambda b, i, j: (0, j)),
            pl.BlockSpec((FF_TILE, D_MODEL), lambda b, i, j: (j, 0)),
            _const_spec((1, D_MODEL)),
        ],
        out_specs=pl.BlockSpec((1, FF_ROW_TILE, D_MODEL), lambda b, i, j: (b, i, 0)),
        out_shape=jax.ShapeDtypeStruct((bsz, seq, D_MODEL), F32),
        scratch_shapes=[pltpu.VMEM((FF_ROW_TILE, D_MODEL), F32)],
        compiler_params=_params("parallel", "parallel", "arbitrary"),
        name="ffn",
    )(h2, x1, mod, w_gate, w_up, w_down, g_final)


def _reorder_head_dims(a):
    lead = a.shape[:-1]
    a = a.reshape(*lead, -1, 2, 2, HEAD_DIM // 4)
    return jnp.swapaxes(a, -3, -2).reshape(*lead, -1)


def _reorder_qk_columns(a):
    n_qk = (N_HEADS + N_KV_HEADS) * HEAD_DIM
    return jnp.concatenate([_reorder_head_dims(a[..., :n_qk]), a[..., n_qk:]], axis=-1)


def _layer(x, mod, tabs, p):
    cos, sin = tabs
    qt, k, vt, u, kn = _in_proj(x, mod, p["g1"], p["w_in"], p["b_in"], p["gq"], p["gk"], cos, sin)
    attn = _attention(qt, k, vt, kn)
    x1, h2 = _out_proj(attn, u, x, mod, p["w_dw"], p["b_dw"], p["g_cn"], p["b_cn"], p["w_out"], p["g2"])
    return _ffn(h2, x1, mod, p["w_gate"], p["w_up"], p["w_down"], p["g_final"])


def kernel(x_prompt, x_sample, c_prompt, c_sample, w_ada, b_ada, g_norm1, w_in, b_in, g_q, g_k, w_dw, b_dw,
           g_cn, b_cn, w_out, g_norm2, w_gate, w_up, w_down, g_final):
    assert w_ada.shape[0] == 1, "single layer"
    assert x_prompt.shape[1] == x_sample.shape[1]
    seq = x_prompt.shape[1]
    assert seq % FF_ROW_TILE == 0 and seq % Q_TILE == 0 and seq % GRID_W == 0
    nb_p, nb_s = x_prompt.shape[0], x_sample.shape[0]

    c_all = jnp.concatenate([c_prompt, c_sample], axis=0)
    pad = (-c_all.shape[0]) % SUBLANES
    c_pad = jnp.pad(c_all, ((0, pad), (0, 0)))
    mod = _modulation(c_pad, w_ada[0], b_ada[0][None, :]).reshape(c_pad.shape[0], N_MOD, D_MODEL)

    p = {
        "g1": g_norm1[0][None, :],
        "w_in": _reorder_qk_columns(w_in[0].astype(BF16)),
        "b_in": _reorder_qk_columns(b_in[0])[None, :],
        "gq": _reorder_head_dims(g_q[0])[None, :],
        "gk": _reorder_head_dims(g_k[0])[None, :],
        "w_dw": w_dw[0],
        "b_dw": b_dw[0][None, :],
        "g_cn": g_cn[0][None, :],
        "b_cn": b_cn[0][None, :],
        "w_out": w_out[0].astype(BF16),
        "g2": g_norm2[0][None, :],
        "w_gate": w_gate[0].astype(BF16),
        "w_up": w_up[0].astype(BF16),
        "w_down": w_down[0].astype(BF16),
        "g_final": g_final[None, :],
    }
    tabs = _rope_tables(seq)
    y_prompt = _layer(x_prompt, mod[:nb_p], tabs, p)
    y_sample = _layer(x_sample, mod[nb_p:nb_p + nb_s], tabs, p)
    return (y_prompt, y_sample)
```
